```python
import math
import jax, jax.numpy as jnp
from jax import lax
import numpy as np

D_MODEL = 2048
BATCH = 4
SEQ = 8192
DEPTH = 1

N_META = 16
CHUNK = 64
EPS = 1e-6
HG_HEADS = 8
HG_DK = 128
HG_DV = 128
HG_KEY_W = HG_HEADS * HG_DK
HG_VAL_W = HG_HEADS * HG_DV
GD_HEADS = 8
GD_DK = 128
GD_DV = 128
GD_KEY_W = GD_HEADS * GD_DK
GD_VAL_W = GD_HEADS * GD_DV
CONV_K = 4
GD_CONV_DIM = 2 * GD_KEY_W + GD_VAL_W
D_FF = ((-(-8 * D_MODEL // 3) + 255) // 256) * 256
SPLIT_SIZES = (HG_KEY_W, HG_KEY_W, HG_VAL_W, HG_VAL_W,
               GD_KEY_W, GD_KEY_W, GD_VAL_W, GD_VAL_W, GD_HEADS, GD_HEADS,
               D_MODEL, D_MODEL)
SPLIT_POINTS = tuple(int(s) for s in np.cumsum(SPLIT_SIZES)[:-1])
IN_DIM = int(sum(SPLIT_SIZES))

kernel_name = 'hybrid_hgrn2_gdn_block'


def rms_norm(x, w):
    xf = x.astype(jnp.float32)
    y = xf * lax.rsqrt(jnp.mean(xf * xf, axis=-1, keepdims=True) + EPS)
    return (y * w.astype(jnp.float32)).astype(x.dtype)


def to_heads(t, heads, d):
    bsz, length, _ = t.shape
    return t.reshape(bsz, length, heads, d).transpose(0, 2, 1, 3)


def l2_normalize(t):
    return t * lax.rsqrt(jnp.sum(t * t, axis=-1, keepdims=True) + EPS)


def gated_head_norm(o, gate, w):
    bsz, heads, length, d = o.shape
    o = jnp.swapaxes(o, 1, 2)
    o = o * lax.rsqrt(jnp.mean(o * o, axis=-1, keepdims=True) + EPS) * w.astype(jnp.float32)
    g = gate.astype(jnp.float32).reshape(bsz, length, heads, d)
    return (o * jax.nn.silu(g)).reshape(bsz, length, heads * d)


def causal_depthwise_conv(x, w):
    channels = x.shape[-1]
    return lax.conv_general_dilated(
        x, w[:, None, :].astype(x.dtype), window_strides=(1,), padding=[(CONV_K - 1, 0)],
        dimension_numbers=('NWC', 'WIO', 'NWC'), feature_group_count=channels)


def run_chunked(step, state, xs):
    meta = tuple(t[:, :, :N_META] for t in xs)
    real = tuple(t[:, :, N_META:] for t in xs)
    state, o_meta = step(state, meta)
    n_chunks = real[0].shape[2] // CHUNK

    def to_chunks(t):
        t = t.reshape(t.shape[:2] + (n_chunks, CHUNK) + t.shape[3:])
        return jnp.moveaxis(t, 2, 0)

    _, o_real = lax.scan(step, state, tuple(to_chunks(t) for t in real))
    o_real = jnp.moveaxis(o_real, 0, 2)
    o_real = o_real.reshape(o_real.shape[:2] + (n_chunks * CHUNK,) + o_real.shape[4:])
    return jnp.concatenate([o_meta, o_real], axis=2)


def hgrn2_chunk(state, inp):
    q, k, v, log_f = inp
    c = q.shape[2]
    b = jnp.cumsum(log_f, axis=2)
    mask = jnp.tril(jnp.ones((c, c), dtype=bool))
    diff = b[:, :, :, None, :] - b[:, :, None, :, :]
    decay = jnp.where(mask[:, :, None], jnp.exp(jnp.minimum(diff, 0.0)), 0.0)
    scores = jnp.einsum('bhtd,bhjd,bhtjd->bhtj', q, k, decay)
    o = (jnp.einsum('bhtd,bhde->bhte', q * jnp.exp(b), state)
         + jnp.einsum('bhtj,bhje->bhte', scores, v))
    b_last = b[:, :, -1:, :]
    new_state = (jnp.exp(b_last[:, :, 0, :])[..., None] * state
                 + jnp.einsum('bhjd,bhje->bhde', k * jnp.exp(b_last - b), v))
    return new_state, o


def gdn_chunk(state, inp):
    q, k, v, g, beta = inp
    c = q.shape[2]
    gc = jnp.cumsum(g, axis=-1)
    diff = gc[..., :, None] - gc[..., None, :]
    rel = jnp.exp(jnp.minimum(diff, 0.0))
    strict = jnp.tril(jnp.ones((c, c), dtype=bool), -1)
    incl = jnp.tril(jnp.ones((c, c), dtype=bool))
    kk = jnp.einsum('bhid,bhjd->bhij', k, k)
    a = jnp.where(strict, beta[..., :, None] * kk * rel, 0.0)
    eye = jnp.eye(c, dtype=q.dtype)
    rhs = jnp.concatenate([(beta * jnp.exp(gc))[..., None] * k, beta[..., None] * v], axis=-1)
    wu = lax.linalg.triangular_solve(eye + a, rhs, left_side=True, lower=True)
    w, u = wu[..., :GD_DK], wu[..., GD_DK:]
    v_new = u - jnp.einsum('bhtd,bhde->bhte', w, state)
    attn = jnp.where(incl, jnp.einsum('bhtd,bhjd->bhtj', q, k) * rel, 0.0)
    o = (jnp.einsum('bhtd,bhde->bhte', q * jnp.exp(gc)[..., None], state)
         + jnp.einsum('bhtj,bhje->bhte', attn, v_new))
    g_last = gc[..., -1:]
    new_state = (jnp.exp(g_last)[..., None] * state
                 + jnp.einsum('bhjd,bhje->bhde', k * jnp.exp(g_last - gc)[..., None], v_new))
    return new_state, o


def hgrn2_mixer(q_raw, f_raw, i_raw, g_raw, lower_bound, norm_w):
    bsz = q_raw.shape[0]
    f32 = jnp.float32
    fs = f_raw.astype(f32)
    lb = lower_bound.astype(f32)
    q = jax.nn.silu(q_raw.astype(f32))
    log_f = jnp.log(lb + (1.0 - lb) * jax.nn.sigmoid(fs))
    k = (1.0 - lb) * jax.nn.sigmoid(-fs)
    v = i_raw.astype(f32)
    q = to_heads(q, HG_HEADS, HG_DK)
    k = to_heads(k, HG_HEADS, HG_DK)
    log_f = to_heads(log_f, HG_HEADS, HG_DK)
    v = to_heads(v, HG_HEADS, HG_DV)
    state0 = jnp.zeros((bsz, HG_HEADS, HG_DK, HG_DV), f32)
    o = run_chunked(hgrn2_chunk, state0, (q, k, v, log_f))
    return gated_head_norm(o, g_raw, norm_w)


def gated_deltanet_mixer(q_raw, k_raw, v_raw, z_raw, a_raw, b_raw, conv_w, a_log, dt_bias, norm_w):
    bsz = q_raw.shape[0]
    f32 = jnp.float32
    qkv = causal_depthwise_conv(jnp.concatenate([q_raw, k_raw, v_raw], axis=-1), conv_w)
    qkv = jax.nn.silu(qkv.astype(f32))
    q = l2_normalize(to_heads(qkv[..., :GD_KEY_W], GD_HEADS, GD_DK)) * (GD_DK ** -0.5)
    k = l2_normalize(to_heads(qkv[..., GD_KEY_W:2 * GD_KEY_W], GD_HEADS, GD_DK))
    v = to_heads(qkv[..., 2 * GD_KEY_W:], GD_HEADS, GD_DV)
    beta = jnp.swapaxes(jax.nn.sigmoid(b_raw.astype(f32)), 1, 2)
    g = -jnp.exp(a_log.astype(f32)) * jax.nn.softplus(a_raw.astype(f32) + dt_bias.astype(f32))
    g = jnp.swapaxes(g, 1, 2)
    state0 = jnp.zeros((bsz, GD_HEADS, GD_DK, GD_DV), f32)
    o = run_chunked(gdn_chunk, state0, (q, k, v, g, beta))
    return gated_head_norm(o, z_raw, norm_w)


def hybrid_mixer(h, norm_w, w_in, lower_bound, hg_norm_w, conv_w, a_log, dt_bias, gd_norm_w,
                 w_branch_a, w_branch_b, w_out):
    xn = rms_norm(h, norm_w)
    proj = jnp.einsum('bld,de->ble', xn, w_in)
    (hq, hf, hi, hg, gq, gk, gv, gz, ga, gb, gate_a, gate_b) = jnp.split(proj, SPLIT_POINTS, axis=-1)
    o_a = hgrn2_mixer(hq, hf, hi, hg, lower_bound, hg_norm_w).astype(h.dtype)
    o_b = gated_deltanet_mixer(gq, gk, gv, gz, ga, gb, conv_w, a_log, dt_bias, gd_norm_w).astype(h.dtype)
    merged = jax.nn.sigmoid(gate_a) * (o_a @ w_branch_a) + jax.nn.sigmoid(gate_b) * (o_b @ w_branch_b)
    return merged @ w_out


def swiglu_ffn(x, w_in, w_out):
    gate, up = jnp.split(x @ w_in, 2, axis=-1)
    return (jax.nn.silu(gate) * up) @ w_out


def setup_inputs(seed: int = 0) -> dict:
    key = jax.random.key(seed)
    ks = jax.random.split(key, 17)
    f32 = jnp.float32

    def nrm(k, shape, scale):
        return jax.random.normal(k, shape, f32) * scale

    x = nrm(ks[0], (BATCH, SEQ, D_MODEL), 1.0)
    meta_tokens = nrm(ks[1], (N_META, D_MODEL), 1.0)
    lb_logits = nrm(ks[2], (DEPTH + 1, HG_KEY_W), 0.5)
    mix_norm_w = 1.0 + nrm(ks[3], (DEPTH, D_MODEL), 0.02)
    w_in = nrm(ks[4], (DEPTH, D_MODEL, IN_DIM), D_MODEL ** -0.5)
    hg_norm_w = 1.0 + nrm(ks[5], (DEPTH, HG_DV), 0.02)
    gd_conv_w = nrm(ks[6], (DEPTH, CONV_K, GD_CONV_DIM), CONV_K ** -0.5)
    gd_a_log = jnp.log(jax.random.uniform(ks[7], (DEPTH, GD_HEADS), f32, 1.0, 16.0))
    dt = jnp.exp(jax.random.uniform(ks[8], (DEPTH, GD_HEADS), f32, math.log(1e-3), math.log(1e-1)))
    gd_dt_bias = dt + jnp.log(-jnp.expm1(-dt))
    gd_norm_w = 1.0 + nrm(ks[9], (DEPTH, GD_DV), 0.02)
    w_branch_a = nrm(ks[10], (DEPTH, HG_VAL_W, D_MODEL), HG_VAL_W ** -0.5)
    w_branch_b = nrm(ks[11], (DEPTH, GD_VAL_W, D_MODEL), GD_VAL_W ** -0.5)
    w_out = nrm(ks[12], (DEPTH, D_MODEL, D_MODEL), D_MODEL ** -0.5)
    ffn_norm_w = 1.0 + nrm(ks[13], (DEPTH, D_MODEL), 0.02)
    w_ffn_in = nrm(ks[14], (DEPTH, D_MODEL, 2 * D_FF), D_MODEL ** -0.5)
    w_ffn_out = nrm(ks[15], (DEPTH, D_FF, D_MODEL), D_FF ** -0.5)
    final_norm_w = 1.0 + nrm(ks[16], (D_MODEL,), 0.02)
    return {'x': x, 'meta_tokens': meta_tokens, 'lb_logits': lb_logits, 'mix_norm_w': mix_norm_w,
            'w_in': w_in, 'hg_norm_w': hg_norm_w, 'gd_conv_w': gd_conv_w, 'gd_a_log': gd_a_log,
            'gd_dt_bias': gd_dt_bias, 'gd_norm_w': gd_norm_w, 'w_branch_a': w_branch_a,
            'w_branch_b': w_branch_b, 'w_out': w_out, 'ffn_norm_w': ffn_norm_w,
            'w_ffn_in': w_ffn_in, 'w_ffn_out': w_ffn_out, 'final_norm_w': final_norm_w}


def reference(x, meta_tokens, lb_logits, mix_norm_w, w_in, hg_norm_w, gd_conv_w, gd_a_log,
              gd_dt_bias, gd_norm_w, w_branch_a, w_branch_b, w_out, ffn_norm_w, w_ffn_in,
              w_ffn_out, final_norm_w):
    bsz = x.shape[0]
    meta = jnp.broadcast_to(meta_tokens[None].astype(x.dtype), (bsz, N_META, D_MODEL))
    h = jnp.concatenate([meta, x], axis=1)
    lower_bounds = jnp.cumsum(jax.nn.softmax(lb_logits.astype(jnp.float32), axis=0), axis=0)
    for layer in range(DEPTH):
        h = h + hybrid_mixer(h, mix_norm_w[layer], w_in[layer], lower_bounds[layer],
                             hg_norm_w[layer], gd_conv_w[layer], gd_a_log[layer],
                             gd_dt_bias[layer], gd_norm_w[layer], w_branch_a[layer],
                             w_branch_b[layer], w_out[layer])
        if layer == DEPTH - 1:
            h = h[:, N_META:]
        h = h + swiglu_ffn(rms_norm(h, ffn_norm_w[layer]), w_ffn_in[layer], w_ffn_out[layer])
    return rms_norm(h, final_norm_w)
```

```python
import functools

import numpy as np
import jax
import jax.numpy as jnp
from jax import lax
from jax.experimental import pallas as pl
from jax.experimental.pallas import tpu as pltpu

F32 = jnp.float32
BF16 = jnp.bfloat16

EPS = 1e-6
N_META = 16
CHUNK = 64
HEADS = 8
HEAD_DIM = 128
MIX_W = HEADS * HEAD_DIM
CONV_K = 4
CONV_TAIL = 8
LANES = 128
VMEM_LIMIT = 56 * 1024 * 1024

LEVELS = tuple(CHUNK >> (i + 1) for i in range(CHUNK.bit_length() - 1))


def _pair_level_table():
    t = np.arange(CHUNK)[:, None]
    j = np.arange(CHUNK)[None, :]
    lvl = np.where(j > t, -1, 0).astype(np.int32)
    for idx, m in enumerate(LEVELS):
        pair = (t // (2 * m) == j // (2 * m)) & (t // m != j // m) & (j < t)
        lvl = np.where(pair, idx + 1, lvl)
    return lvl


def _segment_sum_matrix():
    t = np.arange(CHUNK)[:, None]
    i = np.arange(CHUNK)[None, :]
    blocks = [i <= t, i > t]
    for m in LEVELS:
        if m == 1:
            continue
        mid = (t // (2 * m)) * (2 * m) + m
        upper = (t % (2 * m)) >= m
        blocks.append(np.where(upper, (i >= mid) & (i <= t), (i > t) & (i < mid)))
    return np.concatenate(blocks, axis=0).astype(np.float32)


def _cumsum_matrix():
    t = np.arange(CHUNK)[:, None]
    i = np.arange(CHUNK)[None, :]
    return np.concatenate([i <= t, i > t], axis=0).astype(np.float32)


def _dot(a, b):
    return jnp.dot(a.astype(BF16), b.astype(BF16), preferred_element_type=F32)


def _dot_nt(a, b):
    return lax.dot_general(a.astype(BF16), b.astype(BF16), (((1,), (1,)), ((), ())),
                           preferred_element_type=F32)


def _dot_tn(a, b):
    return lax.dot_general(a.astype(BF16), b.astype(BF16), (((0,), (0,)), ((), ())),
                           preferred_element_type=F32)


def _split2(x):
    hi = x.astype(BF16)
    lo = (x - hi.astype(F32)).astype(BF16)
    return hi, lo


def _dot3(a, b):
    ah, al = _split2(a)
    bh, bl = _split2(b)
    d = functools.partial(jnp.dot, preferred_element_type=F32)
    return d(ah, bh) + (d(ah, bl) + d(al, bh))


def _exact_left_dot(mat01, x):
    hi = x.astype(BF16)
    r1 = x - hi.astype(F32)
    mid = r1.astype(BF16)
    lo = (r1 - mid.astype(F32)).astype(BF16)
    d = functools.partial(jnp.dot, preferred_element_type=F32)
    return d(mat01, hi) + (d(mat01, mid) + d(mat01, lo))


def _silu(x):
    return x * jax.nn.sigmoid(x)


def _softplus(x):
    return jnp.maximum(x, 0.0) + jnp.log(1.0 + jnp.exp(-jnp.abs(x)))


def _rms_norm_rows(x, w):
    return x * lax.rsqrt(jnp.mean(x * x, axis=-1, keepdims=True) + EPS) * w


def _in_proj_kernel(x_ref, nw_ref, w_ref, ws_ref, o_ref, os_ref, xn_ref):
    @pl.when(pl.program_id(1) == 0)
    def _():
        xn_ref[...] = _rms_norm_rows(x_ref[...], nw_ref[...]).astype(BF16)
        os_ref[...] = jnp.dot(xn_ref[...], ws_ref[...], preferred_element_type=F32)

    o_ref[...] = jnp.dot(xn_ref[...], w_ref[...], preferred_element_type=F32)


def _in_proj(x2d, norm_w, w_main, w_small, tm, tn):
    m, d = x2d.shape
    n = w_main.shape[1]
    ns = w_small.shape[1]
    return pl.pallas_call(
        _in_proj_kernel,
        grid=(m // tm, n // tn),
        in_specs=[
            pl.BlockSpec((tm, d), lambda i, j: (i, 0)),
            pl.BlockSpec((1, d), lambda i, j: (0, 0)),
            pl.BlockSpec((d, tn), lambda i, j: (0, j)),
            pl.BlockSpec((d, ns), lambda i, j: (0, 0)),
        ],
        out_specs=[
            pl.BlockSpec((tm, tn), lambda i, j: (i, j)),
            pl.BlockSpec((tm, ns), lambda i, j: (i, 0)),
        ],
        out_shape=[jax.ShapeDtypeStruct((m, n), F32), jax.ShapeDtypeStruct((m, ns), F32)],
        scratch_shapes=[pltpu.VMEM((tm, d), BF16)],
        compiler_params=pltpu.CompilerParams(
            dimension_semantics=("arbitrary", "arbitrary"), vmem_limit_bytes=VMEM_LIMIT),
        name="in_proj",
    )(x2d, norm_w, w_main, w_small)


def _unit_lower_inverse(a, lvl):
    eye = jnp.where(lvl == 0, 1.0, 0.0)
    t = eye - jnp.where(lvl == len(LEVELS), a, 0.0)
    for idx in range(len(LEVELS) - 2, -1, -1):
        a_off = jnp.where(lvl == idx + 1, a, 0.0)
        t = t - _dot3(t, _dot3(a_off, t))
    return t


def _head_norm_gate(o, gate_raw, w):
    return _rms_norm_rows(o, w) * _silu(gate_raw)


def _hgrn2_head(q_raw, f_raw, v, lb, row_valid, seg_mat, lvl, st_ref, h):
    q = _silu(q_raw)
    f = lb + (1.0 - lb) * jax.nn.sigmoid(f_raw)
    k = (1.0 - lb) * jax.nn.sigmoid(-f_raw)
    log_f = jnp.log(f)
    if row_valid is not None:
        k = jnp.where(row_valid, k, 0.0)
        log_f = jnp.where(row_valid, log_f, 0.0)
        f = jnp.where(row_valid, f, 1.0)
    ex = jnp.exp(_exact_left_dot(seg_mat, log_f))
    e_cum = ex[0:CHUNK]
    e_rest = ex[CHUNK:2 * CHUNK]

    scores = jnp.where(lvl == 0, _dot_nt(q, k), 0.0)
    blk = 2
    for idx, m in enumerate(LEVELS):
        if m == 1:
            q_l, k_l = q * f, k
        else:
            e_l = ex[blk * CHUNK:(blk + 1) * CHUNK]
            blk += 1
            q_l, k_l = q * e_l, k * e_l
        scores = scores + jnp.where(lvl == idx + 1, _dot_nt(q_l, k_l), 0.0)

    s_t = st_ref[h]
    o = _dot_nt(q * e_cum, s_t) + _dot(scores, v)
    st_ref[h] = s_t * e_cum[CHUNK - 1:CHUNK, :] + _dot_tn(v, k * e_rest)
    return o


def _gdn_head(qc, kc, vc, beta_c, gc_c, gc_row, egc_c, erest_c, e_last, lvl, s_ref, h):
    q = _silu(qc)
    k = _silu(kc)
    v = _silu(vc)
    q = q * (lax.rsqrt(jnp.sum(q * q, axis=-1, keepdims=True) + EPS) * (HEAD_DIM ** -0.5))
    k = k * lax.rsqrt(jnp.sum(k * k, axis=-1, keepdims=True) + EPS)

    qk_kk = _dot_nt(jnp.concatenate([q, k], axis=0), k)
    qk = qk_kk[0:CHUNK]
    kk = qk_kk[CHUNK:2 * CHUNK]
    rel = jnp.exp(jnp.minimum(gc_c - gc_row, 0.0))
    a = jnp.where(lvl > 0, beta_c * kk * rel, 0.0)
    attn = jnp.where(lvl >= 0, qk * rel, 0.0)

    t_inv = _unit_lower_inverse(a, lvl)
    rhs = jnp.concatenate([(beta_c * egc_c) * k, beta_c * v], axis=1)
    wu = _dot3(t_inv, rhs)
    w = wu[:, 0:HEAD_DIM]
    u = wu[:, HEAD_DIM:2 * HEAD_DIM]

    s = s_ref[h]
    v_new = u - _dot(w, s)
    o = _dot(q * egc_c, s) + _dot(attn, v_new)
    s_ref[h] = s * e_last + _dot_tn(k * erest_c, v_new)
    return o


def _recurrence_kernel(n_pad, emit_state, *refs):
    (p_ref, sm_ref, lbl_ref, hgw_ref, cw_ref, alog_ref, dtb_ref, gdw_ref,
     seg_ref, cum_ref, lvl_ref, shg0_ref, sgd0_ref, tail0_ref) = refs[:14]
    if emit_state:
        oa_ref, ob_ref, shg_out, sgd_out, tail_out, shg_ref, sgd_ref, cbuf_ref = refs[14:]
    else:
        oa_ref, ob_ref, shg_ref, sgd_ref, cbuf_ref = refs[14:]
    c = pl.program_id(1)

    @pl.when(c == 0)
    def _():
        shg_ref[...] = shg0_ref[...]
        sgd_ref[...] = sgd0_ref[...]
        cbuf_ref[0:CONV_TAIL, :] = tail0_ref[...]

    lvl = lvl_ref[...]
    seg_mat = seg_ref[...]
    if n_pad:
        row_valid = lax.broadcasted_iota(jnp.int32, (CHUNK, 1), 0) >= n_pad
    else:
        row_valid = None

    ll = lbl_ref[...]
    ee = jnp.exp(ll - jnp.max(ll, axis=0, keepdims=True))
    lb_all = ee[0:1, :] / jnp.sum(ee, axis=0, keepdims=True)

    hgw = hgw_ref[...]
    for h in range(HEADS):
        lo, hi = h * HEAD_DIM, (h + 1) * HEAD_DIM
        o = _hgrn2_head(p_ref[0, :, lo:hi], p_ref[0, :, MIX_W + lo:MIX_W + hi],
                        p_ref[0, :, 2 * MIX_W + lo:2 * MIX_W + hi], lb_all[:, lo:hi],
                        row_valid, seg_mat, lvl, shg_ref, h)
        gate = p_ref[0, :, 3 * MIX_W + lo:3 * MIX_W + hi]
        oa_ref[0, :, lo:hi] = _head_norm_gate(o, gate, hgw).astype(oa_ref.dtype)

    gd0 = 4 * MIX_W
    cbuf_ref[CONV_TAIL:CONV_TAIL + CHUNK, :] = p_ref[0, :, gd0:gd0 + 3 * MIX_W]
    cw = cw_ref[...]
    first = CONV_TAIL - (CONV_K - 1)
    conv = cw[0:1, :] * cbuf_ref[first:first + CHUNK, :]
    for kk_ in range(1, CONV_K):
        conv = conv + cw[kk_:kk_ + 1, :] * cbuf_ref[first + kk_:first + kk_ + CHUNK, :]
    cbuf_ref[0:CONV_TAIL, :] = cbuf_ref[CHUNK:CHUNK + CONV_TAIL, :]

    a_raw = sm_ref[0, :, 0:LANES]
    b_raw = sm_ref[0, :, LANES:2 * LANES]
    beta_all = jax.nn.sigmoid(b_raw)
    g_all = -jnp.exp(alog_ref[...]) * _softplus(a_raw + dtb_ref[...])
    if row_valid is not None:
        beta_all = jnp.where(row_valid, beta_all, 0.0)
        g_all = jnp.where(row_valid, g_all, 0.0)
    gsum = _exact_left_dot(cum_ref[...], g_all)
    gc_all = gsum[0:CHUNK]
    grest_all = gsum[CHUNK:2 * CHUNK]
    gc_rows = gc_all.T
    egc_all = jnp.exp(gc_all)
    erest_all = jnp.exp(grest_all)

    gdw = gdw_ref[...]
    for h in range(HEADS):
        lo, hi = h * HEAD_DIM, (h + 1) * HEAD_DIM
        o = _gdn_head(conv[:, lo:hi], conv[:, MIX_W + lo:MIX_W + hi],
                      conv[:, 2 * MIX_W + lo:2 * MIX_W + hi],
                      beta_all[:, h:h + 1], gc_all[:, h:h + 1], gc_rows[h:h + 1, :],
                      egc_all[:, h:h + 1], erest_all[:, h:h + 1],
                      egc_all[CHUNK - 1:CHUNK, h:h + 1], lvl, sgd_ref, h)
        gate = p_ref[0, :, gd0 + 3 * MIX_W + lo:gd0 + 3 * MIX_W + hi]
        ob_ref[0, :, lo:hi] = _head_norm_gate(o, gate, gdw).astype(ob_ref.dtype)

    if emit_state:
        @pl.when(c == pl.num_programs(1) - 1)
        def _():
            shg_out[...] = shg_ref[...]
            sgd_out[...] = sgd_ref[...]
            tail_out[...] = cbuf_ref[0:CONV_TAIL, :]


def _recurrence(proj, small, lb_logits, hg_norm_w, conv_w, a_log, dt_bias, gd_norm_w,
                shg0, sgd0, tail0, n_pad, emit_state):
    bsz, length, _ = proj.shape
    n_chunks = length // CHUNK
    seg = jnp.asarray(_segment_sum_matrix(), BF16)
    cum = jnp.asarray(_cumsum_matrix(), BF16)
    lvl = jnp.asarray(_pair_level_table())
    const = lambda *shape: pl.BlockSpec(shape, lambda b, c: (0,) * len(shape))
    state_shape = (HEADS, HEAD_DIM, HEAD_DIM)
    in_specs = [
        pl.BlockSpec((1, CHUNK, 8 * MIX_W), lambda b, c: (b, c, 0)),
        pl.BlockSpec((1, CHUNK, 2 * LANES), lambda b, c: (b, c, 0)),
        const(*lb_logits.shape), const(1, HEAD_DIM), const(CONV_K, 3 * MIX_W),
        const(1, LANES), const(1, LANES), const(1, HEAD_DIM),
        const(*seg.shape), const(*cum.shape), const(CHUNK, CHUNK),
        const(*state_shape), const(*state_shape), const(CONV_TAIL, 3 * MIX_W),
    ]
    out_specs = [
        pl.BlockSpec((1, CHUNK, MIX_W), lambda b, c: (b, c, 0)),
        pl.BlockSpec((1, CHUNK, MIX_W), lambda b, c: (b, c, 0)),
    ]
    out_shape = [jax.ShapeDtypeStruct((bsz, length, MIX_W), BF16),
                 jax.ShapeDtypeStruct((bsz, length, MIX_W), BF16)]
    if emit_state:
        out_specs += [const(*state_shape), const(*state_shape), const(CONV_TAIL, 3 * MIX_W)]
        out_shape += [jax.ShapeDtypeStruct(state_shape, F32), jax.ShapeDtypeStruct(state_shape, F32),
                      jax.ShapeDtypeStruct((CONV_TAIL, 3 * MIX_W), F32)]
    return pl.pallas_call(
        functools.partial(_recurrence_kernel, n_pad, emit_state),
        grid=(bsz, n_chunks),
        in_specs=in_specs,
        out_specs=out_specs,
        out_shape=out_shape,
        scratch_shapes=[pltpu.VMEM(state_shape, F32), pltpu.VMEM(state_shape, F32),
                        pltpu.VMEM((CONV_TAIL + CHUNK, 3 * MIX_W), F32)],
        compiler_params=pltpu.CompilerParams(
            dimension_semantics=("arbitrary", "arbitrary"), vmem_limit_bytes=VMEM_LIMIT),
        name="recurrence_meta" if emit_state else "recurrence",
    )(proj, small, lb_logits, hg_norm_w, conv_w, a_log, dt_bias, gd_norm_w,
      seg, cum, lvl, shg0, sgd0, tail0)


def _merge_kernel(oa_ref, ob_ref, g_ref, x_ref, wa_ref, wb_ref, wo_ref, nw_ref, h_ref, xn_ref):
    d = x_ref.shape[1]
    ya = jnp.dot(oa_ref[...], wa_ref[...], preferred_element_type=F32)
    yb = jnp.dot(ob_ref[...], wb_ref[...], preferred_element_type=F32)
    merged = jax.nn.sigmoid(g_ref[:, 0:d]) * ya + jax.nn.sigmoid(g_ref[:, d:2 * d]) * yb
    h = x_ref[...] + jnp.dot(merged.astype(BF16), wo_ref[...], preferred_element_type=F32)
    h_ref[...] = h
    xn_ref[...] = _rms_norm_rows(h, nw_ref[...]).astype(BF16)


def _merge(oa, ob, proj, gate_block, x2d, wa, wb, wo, ffn_norm_w, tm):
    m, d = x2d.shape
    single = dict(pipeline_mode=pl.Buffered(1))
    return pl.pallas_call(
        _merge_kernel,
        grid=(m // tm,),
        in_specs=[
            pl.BlockSpec((tm, MIX_W), lambda i: (i, 0)),
            pl.BlockSpec((tm, MIX_W), lambda i: (i, 0)),
            pl.BlockSpec((tm, 2 * d), lambda i: (i, gate_block)),
            pl.BlockSpec((tm, d), lambda i: (i, 0)),
            pl.BlockSpec((MIX_W, d), lambda i: (0, 0), **single),
            pl.BlockSpec((MIX_W, d), lambda i: (0, 0), **single),
            pl.BlockSpec((d, d), lambda i: (0, 0), **single),
            pl.BlockSpec((1, d), lambda i: (0, 0)),
        ],
        out_specs=[pl.BlockSpec((tm, d), lambda i: (i, 0)), pl.BlockSpec((tm, d), lambda i: (i, 0))],
        out_shape=[jax.ShapeDtypeStruct((m, d), F32), jax.ShapeDtypeStruct((m, d), BF16)],
        compiler_params=pltpu.CompilerParams(
            dimension_semantics=("arbitrary",), vmem_limit_bytes=VMEM_LIMIT),
        name="merge",
    )(oa, ob, proj, x2d, wa, wb, wo, ffn_norm_w)


def _ffn_kernel(xn_ref, h_ref, wg_ref, wu_ref, wd_ref, nw_ref, o_ref):
    j = pl.program_id(1)
    xn = xn_ref[...]
    gate = jnp.dot(xn, wg_ref[...], preferred_element_type=F32)
    up = jnp.dot(xn, wu_ref[...], preferred_element_type=F32)
    part = jnp.dot((_silu(gate) * up).astype(BF16), wd_ref[...], preferred_element_type=F32)

    @pl.when(j == 0)
    def _():
        o_ref[...] = h_ref[...] + part

    @pl.when(j > 0)
    def _():
        o_ref[...] += part

    @pl.when(j == pl.num_programs(1) - 1)
    def _():
        o_ref[...] = _rms_norm_rows(o_ref[...], nw_ref[...])


def _ffn(xn, h, w_gate_up, w_down, final_norm_w, tm, tf):
    m, d = h.shape
    ff = w_down.shape[0]
    nf = ff // tf
    return pl.pallas_call(
        _ffn_kernel,
        grid=(m // tm, nf),
        in_specs=[
            pl.BlockSpec((tm, d), lambda i, j: (i, 0)),
            pl.BlockSpec((tm, d), lambda i, j: (i, 0)),
            pl.BlockSpec((d, tf), lambda i, j: (0, j)),
            pl.BlockSpec((d, tf), lambda i, j: (0, nf + j)),
            pl.BlockSpec((tf, d), lambda i, j: (j, 0)),
            pl.BlockSpec((1, d), lambda i, j: (0, 0)),
        ],
        out_specs=pl.BlockSpec((tm, d), lambda i, j: (i, 0)),
        out_shape=jax.ShapeDtypeStruct((m, d), F32),
        compiler_params=pltpu.CompilerParams(
            dimension_semantics=("arbitrary", "arbitrary"), vmem_limit_bytes=VMEM_LIMIT),
        name="ffn",
    )(xn, h, w_gate_up, w_gate_up, w_down, final_norm_w)


def _pad_lanes(v):
    return jnp.pad(v.astype(F32), (0, LANES - v.shape[0]))[None, :]


def kernel(x, meta_tokens, lb_logits, mix_norm_w, w_in, hg_norm_w, gd_conv_w, gd_a_log, gd_dt_bias,
           gd_norm_w, w_branch_a, w_branch_b, w_out, ffn_norm_w, w_ffn_in, w_ffn_out, final_norm_w):
    bsz, seq, d = x.shape
    layer = 0
    assert mix_norm_w.shape[0] == 1, "single-layer block"
    assert seq % CHUNK == 0 and meta_tokens.shape[0] == N_META <= CHUNK

    w = w_in[layer]
    n_mix = 8 * MIX_W
    w_main = jnp.concatenate([w[:, :n_mix], w[:, n_mix + 2 * HEADS:]], axis=1).astype(BF16)
    pad = jnp.zeros((d, LANES - HEADS), w.dtype)
    w_small = jnp.concatenate([w[:, n_mix:n_mix + HEADS], pad,
                               w[:, n_mix + HEADS:n_mix + 2 * HEADS], pad], axis=1).astype(BF16)
    norm_w = mix_norm_w[layer][None, :]
    rec_params = (lb_logits.astype(F32), hg_norm_w[layer][None, :], gd_conv_w[layer],
                  _pad_lanes(gd_a_log[layer]), _pad_lanes(gd_dt_bias[layer]), gd_norm_w[layer][None, :])

    n_pad = CHUNK - N_META
    x_meta = jnp.concatenate([jnp.zeros((n_pad, d), x.dtype), meta_tokens.astype(x.dtype)], axis=0)
    proj_m, small_m = _in_proj(x_meta, norm_w, w_main, w_small, tm=CHUNK, tn=1024)
    zero_state = jnp.zeros((HEADS, HEAD_DIM, HEAD_DIM), F32)
    zero_tail = jnp.zeros((CONV_TAIL, 3 * MIX_W), F32)
    _, _, shg0, sgd0, tail0 = _recurrence(proj_m[None], small_m[None], *rec_params,
                                          zero_state, zero_state, zero_tail, n_pad=n_pad, emit_state=True)

    x2d = x.reshape(bsz * seq, d)
    proj, small = _in_proj(x2d, norm_w, w_main, w_small, tm=min(512, bsz * seq), tn=1024)
    oa, ob = _recurrence(proj.reshape(bsz, seq, -1), small.reshape(bsz, seq, -1), *rec_params,
                         shg0, sgd0, tail0, n_pad=0, emit_state=False)

    h1, xn2 = _merge(oa.reshape(bsz * seq, MIX_W), ob.reshape(bsz * seq, MIX_W), proj,
                     n_mix // (2 * d), x2d, w_branch_a[layer].astype(BF16), w_branch_b[layer].astype(BF16),
                     w_out[layer].astype(BF16), ffn_norm_w[layer][None, :], tm=min(256, bsz * seq))
    out = _ffn(xn2, h1, w_ffn_in[layer].astype(BF16), w_ffn_out[layer].astype(BF16),
               final_norm_w[None, :], tm=min(512, bsz * seq), tf=512)
    return out.reshape(bsz, seq, d)
```

```python
import functools

import numpy as np
import jax
import jax.numpy as jnp
from jax import lax
from jax.experimental import pallas as pl
from jax.experimental.pallas import tpu as pltpu

F32 = jnp.float32
BF16 = jnp.bfloat16

EPS = 1e-6
N_META = 16
CHUNK = 64
HEADS = 8
HEAD_DIM = 128
MIX_W = HEADS * HEAD_DIM
CONV_K = 4
CONV_TAIL = 8
LANES = 128
VMEM_LIMIT = 56 * 1024 * 1024

LEVELS = tuple(CHUNK >> (i + 1) for i in range(CHUNK.bit_length() - 1))


def _pair_level_table():
    t = np.arange(CHUNK)[:, None]
    j = np.arange(CHUNK)[None, :]
    lvl = np.where(j > t, -1, 0).astype(np.int32)
    for idx, m in enumerate(LEVELS):
        pair = (t // (2 * m) == j // (2 * m)) & (t // m != j // m) & (j < t)
        lvl = np.where(pair, idx + 1, lvl)
    return lvl


def _segment_sum_matrix():
    t = np.arange(CHUNK)[:, None]
    i = np.arange(CHUNK)[None, :]
    blocks = [i <= t, i > t]
    for m in LEVELS:
        if m == 1:
            continue
        mid = (t // (2 * m)) * (2 * m) + m
        upper = (t % (2 * m)) >= m
        blocks.append(np.where(upper, (i >= mid) & (i <= t), (i > t) & (i < mid)))
    return np.concatenate(blocks, axis=0).astype(np.float32)


def _cumsum_matrix():
    t = np.arange(CHUNK)[:, None]
    i = np.arange(CHUNK)[None, :]
    return np.concatenate([i <= t, i > t], axis=0).astype(np.float32)


def _bdot(a, b):
    return jnp.dot(a, b, preferred_element_type=F32)


def _dot(a, b):
    return _bdot(a.astype(BF16), b.astype(BF16))


def _dot_nt(a, b):
    return lax.dot_general(a.astype(BF16), b.astype(BF16), (((1,), (1,)), ((), ())),
                           preferred_element_type=F32)


def _dot_tn(a, b):
    return lax.dot_general(a.astype(BF16), b.astype(BF16), (((0,), (0,)), ((), ())),
                           preferred_element_type=F32)


def _exact_left_dot(mat01, x, terms):
    n = x.shape[1]
    pieces = []
    r = x
    for _ in range(terms):
        p = r.astype(BF16)
        pieces.append(p)
        r = r - p.astype(F32)
    y = _bdot(mat01, jnp.concatenate(pieces, axis=1))
    out = y[:, 0:n]
    for i in range(1, terms):
        out = out + y[:, i * n:(i + 1) * n]
    return out


def _silu(x):
    return x * jax.nn.sigmoid(x)


def _softplus(x):
    return jnp.maximum(x, 0.0) + jnp.log(1.0 + jnp.exp(-jnp.abs(x)))


def _rms_norm_rows(x, w):
    return x * lax.rsqrt(jnp.mean(x * x, axis=-1, keepdims=True) + EPS) * w


def _select_by_level(masks, parts):
    out = jnp.where(masks[-1], parts[-1], 0.0)
    for m, p in zip(masks[-2::-1], parts[-2::-1]):
        out = jnp.where(m, p, out)
    return out


def _in_proj_kernel(x_ref, nw_ref, w_ref, ws_ref, o_ref, os_ref, xn_ref):
    @pl.when(pl.program_id(1) == 0)
    def _():
        xn_ref[...] = _rms_norm_rows(x_ref[...], nw_ref[...]).astype(BF16)
        os_ref[...] = jnp.dot(xn_ref[...], ws_ref[...], preferred_element_type=F32)

    o_ref[...] = jnp.dot(xn_ref[...], w_ref[...], preferred_element_type=F32)


def _in_proj(x2d, norm_w, w_main, w_small, tm, tn):
    m, d = x2d.shape
    n = w_main.shape[1]
    ns = w_small.shape[1]
    return pl.pallas_call(
        _in_proj_kernel,
        grid=(m // tm, n // tn),
        in_specs=[
            pl.BlockSpec((tm, d), lambda i, j: (i, 0)),
            pl.BlockSpec((1, d), lambda i, j: (0, 0)),
            pl.BlockSpec((d, tn), lambda i, j: (0, j)),
            pl.BlockSpec((d, ns), lambda i, j: (0, 0)),
        ],
        out_specs=[
            pl.BlockSpec((tm, tn), lambda i, j: (i, j)),
            pl.BlockSpec((tm, ns), lambda i, j: (i, 0)),
        ],
        out_shape=[jax.ShapeDtypeStruct((m, n), F32), jax.ShapeDtypeStruct((m, ns), F32)],
        scratch_shapes=[pltpu.VMEM((tm, d), BF16)],
        compiler_params=pltpu.CompilerParams(
            dimension_semantics=("arbitrary", "arbitrary"), vmem_limit_bytes=VMEM_LIMIT),
        name="in_proj",
    )(x2d, norm_w, w_main, w_small)


def _unit_lower_inverse_minus_eye(a_list, masks):
    e_list = [-jnp.where(masks[len(LEVELS)], a, 0.0) for a in a_list]
    for idx in range(len(LEVELS) - 2, -1, -1):
        off = [jnp.where(masks[idx + 1], a, 0.0) for a in a_list]
        x_list = [o + _dot(o, e) for o, e in zip(off, e_list)]
        e_list = [e - x - _dot(e, x) for e, x in zip(e_list, x_list)]
    return e_list


def _hgrn2_intra(p_ref, rows, lb_all, row_valid, seg_mat, masks):
    tasks = []
    for r0, valid in zip(rows, row_valid):
        for h in range(HEADS):
            lo, hi = h * HEAD_DIM, (h + 1) * HEAD_DIM
            lb = lb_all[:, lo:hi]
            q = _silu(p_ref[0, r0:r0 + CHUNK, lo:hi])
            f_raw = p_ref[0, r0:r0 + CHUNK, MIX_W + lo:MIX_W + hi]
            f = lb + (1.0 - lb) * jax.nn.sigmoid(f_raw)
            k = (1.0 - lb) * jax.nn.sigmoid(-f_raw)
            log_f = jnp.log(f)
            if valid is not None:
                k = jnp.where(valid, k, 0.0)
                log_f = jnp.where(valid, log_f, 0.0)
                f = jnp.where(valid, f, 1.0)
            v = p_ref[0, r0:r0 + CHUNK, 2 * MIX_W + lo:2 * MIX_W + hi].astype(BF16)
            tasks.append(dict(q=q, f=f, k=k, log_f=log_f, v=v))
    for t in tasks:
        t["ex"] = jnp.exp(_exact_left_dot(seg_mat, t.pop("log_f"), terms=2))
    for t in tasks:
        q, k, ex = t["q"], t["k"], t["ex"]
        parts = [_dot_nt(q, k)]
        blk = 2
        for m in LEVELS:
            if m == 1:
                parts.append(_dot_nt(q * t["f"], k))
            else:
                e_l = ex[blk * CHUNK:(blk + 1) * CHUNK]
                blk += 1
                parts.append(_dot_nt(q * e_l, k * e_l))
        t["scores"] = _select_by_level(masks, parts).astype(BF16)
        e_cum = ex[0:CHUNK]
        t["q_cum"] = (q * e_cum).astype(BF16)
        t["k_rest"] = (k * ex[CHUNK:2 * CHUNK]).astype(BF16)
        t["e_last"] = e_cum[CHUNK - 1:CHUNK, :]
        del t["ex"], t["q"], t["k"], t["f"]
    return tasks


def _gdn_intra(conv, rows, beta_all, gc_all, gc_rows, egc_all, erest_all, masks):
    strict = masks[1]
    for m in masks[2:]:
        strict = strict | m
    incl = strict | masks[0]
    tasks = []
    for si, r0 in enumerate(rows):
        for h in range(HEADS):
            lo, hi = h * HEAD_DIM, (h + 1) * HEAD_DIM
            q = _silu(conv[r0:r0 + CHUNK, lo:hi])
            k = _silu(conv[r0:r0 + CHUNK, MIX_W + lo:MIX_W + hi])
            v = _silu(conv[r0:r0 + CHUNK, 2 * MIX_W + lo:2 * MIX_W + hi])
            q = q * (lax.rsqrt(jnp.sum(q * q, axis=-1, keepdims=True) + EPS) * (HEAD_DIM ** -0.5))
            k = k * lax.rsqrt(jnp.sum(k * k, axis=-1, keepdims=True) + EPS)
            beta_c = beta_all[r0:r0 + CHUNK, h:h + 1]
            gc_c = gc_all[si][:, h:h + 1]
            egc_c = egc_all[si][:, h:h + 1]
            tasks.append(dict(
                q=q, k=k, beta_c=beta_c,
                rel=jnp.exp(jnp.minimum(gc_c - gc_rows[si][h:h + 1, :], 0.0)),
                rhs=jnp.concatenate([(beta_c * egc_c) * k, beta_c * v], axis=1),
                q_cum=(q * egc_c).astype(BF16),
                k_rest=(k * erest_all[si][:, h:h + 1]).astype(BF16),
                e_last=egc_all[si][CHUNK - 1:CHUNK, h:h + 1]))
    for t in tasks:
        qk_kk = _dot_nt(jnp.concatenate([t.pop("q"), t["k"]], axis=0), t.pop("k"))
        rel = t.pop("rel")
        t["attn"] = jnp.where(incl, qk_kk[0:CHUNK] * rel, 0.0).astype(BF16)
        t["a"] = jnp.where(strict, t.pop("beta_c") * qk_kk[CHUNK:2 * CHUNK] * rel, 0.0)
    e_list = _unit_lower_inverse_minus_eye([t.pop("a") for t in tasks], masks)
    for t, e in zip(tasks, e_list):
        rhs = t.pop("rhs")
        wu = rhs + _dot(e, rhs)
        t["w"] = wu[:, 0:HEAD_DIM].astype(BF16)
        t["u"] = wu[:, HEAD_DIM:2 * HEAD_DIM]
    return tasks


def _recurrence_kernel(n_pad, emit_state, cps, *refs):
    (p_ref, sm_ref, lbl_ref, hgw_ref, cw_ref, alog_ref, dtb_ref, gdw_ref,
     seg_ref, cum_ref, lvl_ref, shg0_ref, sgd0_ref, tail0_ref) = refs[:14]
    if emit_state:
        oa_ref, ob_ref, shg_out, sgd_out, tail_out, shg_ref, sgd_ref, cbuf_ref = refs[14:]
    else:
        oa_ref, ob_ref, shg_ref, sgd_ref, cbuf_ref = refs[14:]
    c = pl.program_id(1)
    n_rows = cps * CHUNK
    rows = [s * CHUNK for s in range(cps)]

    @pl.when(c == 0)
    def _():
        shg_ref[...] = shg0_ref[...]
        sgd_ref[...] = sgd0_ref[...]
        cbuf_ref[0:CONV_TAIL, :] = tail0_ref[...]

    lvl = lvl_ref[...]
    masks = [lvl == i for i in range(len(LEVELS) + 1)]
    seg_mat = seg_ref[...]
    if n_pad:
        row_ids = lax.broadcasted_iota(jnp.int32, (n_rows, 1), 0)
        valid_all = row_ids >= n_pad
        row_valid = [valid_all[r0:r0 + CHUNK] for r0 in rows]
    else:
        valid_all = None
        row_valid = [None] * cps

    ll = lbl_ref[...]
    ee = jnp.exp(ll - jnp.max(ll, axis=0, keepdims=True))
    lb_all = ee[0:1, :] / jnp.sum(ee, axis=0, keepdims=True)

    hg = _hgrn2_intra(p_ref, rows, lb_all, row_valid, seg_mat, masks)

    gd0 = 4 * MIX_W
    cbuf_ref[CONV_TAIL:CONV_TAIL + n_rows, :] = p_ref[0, :, gd0:gd0 + 3 * MIX_W]
    cw = cw_ref[...]
    first = CONV_TAIL - (CONV_K - 1)
    conv = cw[0:1, :] * cbuf_ref[first:first + n_rows, :]
    for tap in range(1, CONV_K):
        conv = conv + cw[tap:tap + 1, :] * cbuf_ref[first + tap:first + tap + n_rows, :]
    cbuf_ref[0:CONV_TAIL, :] = cbuf_ref[n_rows:n_rows + CONV_TAIL, :]

    beta_all = jax.nn.sigmoid(sm_ref[0, :, LANES:2 * LANES])
    g_all = -jnp.exp(alog_ref[...]) * _softplus(sm_ref[0, :, 0:LANES] + dtb_ref[...])
    if valid_all is not None:
        beta_all = jnp.where(valid_all, beta_all, 0.0)
        g_all = jnp.where(valid_all, g_all, 0.0)
    cum_mat = cum_ref[...]
    gsum = [_exact_left_dot(cum_mat, g_all[r0:r0 + CHUNK], terms=3) for r0 in rows]
    gc_all = [g[0:CHUNK] for g in gsum]
    gc_rows = [g.T for g in gc_all]
    egc_all = [jnp.exp(g) for g in gc_all]
    erest_all = [jnp.exp(g[CHUNK:2 * CHUNK]) for g in gsum]

    gd = _gdn_intra(conv, rows, beta_all, gc_all, gc_rows, egc_all, erest_all, masks)

    s_hg = [shg_ref[h] for h in range(HEADS)]
    s_gd = [sgd_ref[h] for h in range(HEADS)]
    hgw = hgw_ref[...]
    gdw = gdw_ref[...]
    for si, r0 in enumerate(rows):
        hg_t = hg[si * HEADS:(si + 1) * HEADS]
        gd_t = gd[si * HEADS:(si + 1) * HEADS]
        o_hg = [lax.dot_general(t["q_cum"], s.astype(BF16), (((1,), (1,)), ((), ())),
                                preferred_element_type=F32) + _bdot(t["scores"], t["v"])
                for t, s in zip(hg_t, s_hg)]
        s_hg = [s * t["e_last"] + _dot_tn(t["v"], t["k_rest"]) for t, s in zip(hg_t, s_hg)]
        wq_s = [_bdot(jnp.concatenate([t["w"], t["q_cum"]], axis=0), s.astype(BF16))
                for t, s in zip(gd_t, s_gd)]
        v_new = [(t["u"] - ws[0:CHUNK]).astype(BF16) for t, ws in zip(gd_t, wq_s)]
        o_gd = [ws[CHUNK:2 * CHUNK] + _bdot(t["attn"], vn) for t, ws, vn in zip(gd_t, wq_s, v_new)]
        s_gd = [s * t["e_last"] + _dot_tn(t["k_rest"], vn) for t, s, vn in zip(gd_t, s_gd, v_new)]
        for h in range(HEADS):
            lo, hi = h * HEAD_DIM, (h + 1) * HEAD_DIM
            gate = p_ref[0, r0:r0 + CHUNK, 3 * MIX_W + lo:3 * MIX_W + hi]
            oa_ref[0, r0:r0 + CHUNK, lo:hi] = (_rms_norm_rows(o_hg[h], hgw) * _silu(gate)).astype(oa_ref.dtype)
            gate = p_ref[0, r0:r0 + CHUNK, gd0 + 3 * MIX_W + lo:gd0 + 3 * MIX_W + hi]
            ob_ref[0, r0:r0 + CHUNK, lo:hi] = (_rms_norm_rows(o_gd[h], gdw) * _silu(gate)).astype(ob_ref.dtype)
    for h in range(HEADS):
        shg_ref[h] = s_hg[h]
        sgd_ref[h] = s_gd[h]

    if emit_state:
        @pl.when(c == pl.num_programs(1) - 1)
        def _():
            shg_out[...] = shg_ref[...]
            sgd_out[...] = sgd_ref[...]
            tail_out[...] = cbuf_ref[0:CONV_TAIL, :]


def _recurrence(proj, small, lb_logits, hg_norm_w, conv_w, a_log, dt_bias, gd_norm_w,
                shg0, sgd0, tail0, n_pad, emit_state, cps):
    bsz, length, _ = proj.shape
    n_rows = cps * CHUNK
    seg = jnp.asarray(_segment_sum_matrix(), BF16)
    cum = jnp.asarray(_cumsum_matrix(), BF16)
    lvl = jnp.asarray(_pair_level_table())
    const = lambda *shape: pl.BlockSpec(shape, lambda b, c: (0,) * len(shape))
    state_shape = (HEADS, HEAD_DIM, HEAD_DIM)
    in_specs = [
        pl.BlockSpec((1, n_rows, 8 * MIX_W), lambda b, c: (b, c, 0)),
        pl.BlockSpec((1, n_rows, 2 * LANES), lambda b, c: (b, c, 0)),
        const(*lb_logits.shape), const(1, HEAD_DIM), const(CONV_K, 3 * MIX_W),
        const(1, LANES), const(1, LANES), const(1, HEAD_DIM),
        const(*seg.shape), const(*cum.shape), const(CHUNK, CHUNK),
        const(*state_shape), const(*state_shape), const(CONV_TAIL, 3 * MIX_W),
    ]
    out_specs = [
        pl.BlockSpec((1, n_rows, MIX_W), lambda b, c: (b, c, 0)),
        pl.BlockSpec((1, n_rows, MIX_W), lambda b, c: (b, c, 0)),
    ]
    out_shape = [jax.ShapeDtypeStruct((bsz, length, MIX_W), BF16),
                 jax.ShapeDtypeStruct((bsz, length, MIX_W), BF16)]
    if emit_state:
        out_specs += [const(*state_shape), const(*state_shape), const(CONV_TAIL, 3 * MIX_W)]
        out_shape += [jax.ShapeDtypeStruct(state_shape, F32), jax.ShapeDtypeStruct(state_shape, F32),
                      jax.ShapeDtypeStruct((CONV_TAIL, 3 * MIX_W), F32)]
    return pl.pallas_call(
        functools.partial(_recurrence_kernel, n_pad, emit_state, cps),
        grid=(bsz, length // n_rows),
        in_specs=in_specs,
        out_specs=out_specs,
        out_shape=out_shape,
        scratch_shapes=[pltpu.VMEM(state_shape, F32), pltpu.VMEM(state_shape, F32),
                        pltpu.VMEM((CONV_TAIL + n_rows, 3 * MIX_W), F32)],
        compiler_params=pltpu.CompilerParams(
            dimension_semantics=("arbitrary", "arbitrary"), vmem_limit_bytes=VMEM_LIMIT),
        name="recurrence_meta" if emit_state else "recurrence",
    )(proj, small, lb_logits, hg_norm_w, conv_w, a_log, dt_bias, gd_norm_w,
      seg, cum, lvl, shg0, sgd0, tail0)


def _merge_kernel(oa_ref, ob_ref, g_ref, x_ref, wa_ref, wb_ref, wo_ref, nw_ref, h_ref, xn_ref):
    d = x_ref.shape[1]
    ya = jnp.dot(oa_ref[...], wa_ref[...], preferred_element_type=F32)
    yb = jnp.dot(ob_ref[...], wb_ref[...], preferred_element_type=F32)
    merged = jax.nn.sigmoid(g_ref[:, 0:d]) * ya + jax.nn.sigmoid(g_ref[:, d:2 * d]) * yb
    h = x_ref[...] + jnp.dot(merged.astype(BF16), wo_ref[...], preferred_element_type=F32)
    h_ref[...] = h
    xn_ref[...] = _rms_norm_rows(h, nw_ref[...]).astype(BF16)


def _merge(oa, ob, proj, gate_block, x2d, wa, wb, wo, ffn_norm_w, tm):
    m, d = x2d.shape
    single = dict(pipeline_mode=pl.Buffered(1))
    return pl.pallas_call(
        _merge_kernel,
        grid=(m // tm,),
        in_specs=[
            pl.BlockSpec((tm, MIX_W), lambda i: (i, 0)),
            pl.BlockSpec((tm, MIX_W), lambda i: (i, 0)),
            pl.BlockSpec((tm, 2 * d), lambda i: (i, gate_block)),
            pl.BlockSpec((tm, d), lambda i: (i, 0)),
            pl.BlockSpec((MIX_W, d), lambda i: (0, 0), **single),
            pl.BlockSpec((MIX_W, d), lambda i: (0, 0), **single),
            pl.BlockSpec((d, d), lambda i: (0, 0), **single),
            pl.BlockSpec((1, d), lambda i: (0, 0)),
        ],
        out_specs=[pl.BlockSpec((tm, d), lambda i: (i, 0)), pl.BlockSpec((tm, d), lambda i: (i, 0))],
        out_shape=[jax.ShapeDtypeStruct((m, d), F32), jax.ShapeDtypeStruct((m, d), BF16)],
        compiler_params=pltpu.CompilerParams(
            dimension_semantics=("arbitrary",), vmem_limit_bytes=VMEM_LIMIT),
        name="merge",
    )(oa, ob, proj, x2d, wa, wb, wo, ffn_norm_w)


def _ffn_kernel(xn_ref, h_ref, wg_ref, wu_ref, wd_ref, nw_ref, o_ref):
    j = pl.program_id(1)
    xn = xn_ref[...]
    gate = jnp.dot(xn, wg_ref[...], preferred_element_type=F32)
    up = jnp.dot(xn, wu_ref[...], preferred_element_type=F32)
    part = jnp.dot((_silu(gate) * up).astype(BF16), wd_ref[...], preferred_element_type=F32)

    @pl.when(j == 0)
    def _():
        o_ref[...] = h_ref[...] + part

    @pl.when(j > 0)
    def _():
        o_ref[...] += part

    @pl.when(j == pl.num_programs(1) - 1)
    def _():
        o_ref[...] = _rms_norm_rows(o_ref[...], nw_ref[...])


def _ffn(xn, h, w_gate_up, w_down, final_norm_w, tm, tf):
    m, d = h.shape
    ff = w_down.shape[0]
    nf = ff // tf
    return pl.pallas_call(
        _ffn_kernel,
        grid=(m // tm, nf),
        in_specs=[
            pl.BlockSpec((tm, d), lambda i, j: (i, 0)),
            pl.BlockSpec((tm, d), lambda i, j: (i, 0)),
            pl.BlockSpec((d, tf), lambda i, j: (0, j)),
            pl.BlockSpec((d, tf), lambda i, j: (0, nf + j)),
            pl.BlockSpec((tf, d), lambda i, j: (j, 0)),
            pl.BlockSpec((1, d), lambda i, j: (0, 0)),
        ],
        out_specs=pl.BlockSpec((tm, d), lambda i, j: (i, 0)),
        out_shape=jax.ShapeDtypeStruct((m, d), F32),
        compiler_params=pltpu.CompilerParams(
            dimension_semantics=("arbitrary", "arbitrary"), vmem_limit_bytes=VMEM_LIMIT),
        name="ffn",
    )(xn, h, w_gate_up, w_gate_up, w_down, final_norm_w)


def _pad_lanes(v):
    return jnp.pad(v.astype(F32), (0, LANES - v.shape[0]))[None, :]


def kernel(x, meta_tokens, lb_logits, mix_norm_w, w_in, hg_norm_w, gd_conv_w, gd_a_log, gd_dt_bias,
           gd_norm_w, w_branch_a, w_branch_b, w_out, ffn_norm_w, w_ffn_in, w_ffn_out, final_norm_w):
    bsz, seq, d = x.shape
    layer = 0
    assert mix_norm_w.shape[0] == 1, "single-layer block"
    assert seq % (2 * CHUNK) == 0 and meta_tokens.shape[0] == N_META <= CHUNK

    w = w_in[layer]
    n_mix = 8 * MIX_W
    w_main = jnp.concatenate([w[:, :n_mix], w[:, n_mix + 2 * HEADS:]], axis=1).astype(BF16)
    pad = jnp.zeros((d, LANES - HEADS), w.dtype)
    w_small = jnp.concatenate([w[:, n_mix:n_mix + HEADS], pad,
                               w[:, n_mix + HEADS:n_mix + 2 * HEADS], pad], axis=1).astype(BF16)
    norm_w = mix_norm_w[layer][None, :]
    rec_params = (lb_logits.astype(F32), hg_norm_w[layer][None, :], gd_conv_w[layer],
                  _pad_lanes(gd_a_log[layer]), _pad_lanes(gd_dt_bias[layer]), gd_norm_w[layer][None, :])

    n_pad = CHUNK - N_META
    x_meta = jnp.concatenate([jnp.zeros((n_pad, d), x.dtype), meta_tokens.astype(x.dtype)], axis=0)
    proj_m, small_m = _in_proj(x_meta, norm_w, w_main, w_small, tm=CHUNK, tn=1024)
    zero_state = jnp.zeros((HEADS, HEAD_DIM, HEAD_DIM), F32)
    zero_tail = jnp.zeros((CONV_TAIL, 3 * MIX_W), F32)
    _, _, shg0, sgd0, tail0 = _recurrence(proj_m[None], small_m[None], *rec_params,
                                          zero_state, zero_state, zero_tail,
                                          n_pad=n_pad, emit_state=True, cps=1)

    x2d = x.reshape(bsz * seq, d)
    proj, small = _in_proj(x2d, norm_w, w_main, w_small, tm=min(1024, bsz * seq), tn=512)
    oa, ob = _recurrence(proj.reshape(bsz, seq, -1), small.reshape(bsz, seq, -1), *rec_params,
                         shg0, sgd0, tail0, n_pad=0, emit_state=False, cps=2)

    h1, xn2 = _merge(oa.reshape(bsz * seq, MIX_W), ob.reshape(bsz * seq, MIX_W), proj,
                     n_mix // (2 * d), x2d, w_branch_a[layer].astype(BF16), w_branch_b[layer].astype(BF16),
                     w_out[layer].astype(BF16), ffn_norm_w[layer][None, :], tm=min(256, bsz * seq))
    out = _ffn(xn2, h1, w_ffn_in[layer].astype(BF16), w_ffn_out[layer].astype(BF16),
               final_norm_w[None, :], tm=min(512, bsz * seq), tf=512)
    return out.reshape(bsz, seq, d)
```

```python
import functools

import numpy as np
import jax
import jax.numpy as jnp
from jax import lax
from jax.experimental import pallas as pl
from jax.experimental.pallas import tpu as pltpu

F32 = jnp.float32
BF16 = jnp.bfloat16

EPS = 1e-6
LOG2_E = 1.4426950408889634
N_META = 16
CHUNK = 64
HEADS = 8
HEAD_DIM = 128
MIX_W = HEADS * HEAD_DIM
CONV_K = 4
CONV_TAIL = 8
LANES = 128
SUBLANES = 8
VMEM_LIMIT = 56 * 1024 * 1024

LEVELS = tuple(CHUNK >> (i + 1) for i in range(CHUNK.bit_length() - 1))


def _pair_level_table():
    t = np.arange(CHUNK)[:, None]
    j = np.arange(CHUNK)[None, :]
    lvl = np.where(j > t, -1, 0).astype(np.int32)
    for idx, m in enumerate(LEVELS):
        pair = (t // (2 * m) == j // (2 * m)) & (t // m != j // m) & (j < t)
        lvl = np.where(pair, idx + 1, lvl)
    return lvl


def _cumsum_matrix():
    t = np.arange(CHUNK)[:, None]
    i = np.arange(CHUNK)[None, :]
    return np.concatenate([i <= t, i > t], axis=0).astype(np.float32)


def _bdot(a, b):
    return jnp.dot(a, b, preferred_element_type=F32)


def _dot(a, b):
    return _bdot(a.astype(BF16), b.astype(BF16))


def _dot_nt(a, b):
    return lax.dot_general(a.astype(BF16), b.astype(BF16), (((1,), (1,)), ((), ())),
                           preferred_element_type=F32)


def _dot_tn(a, b):
    return lax.dot_general(a.astype(BF16), b.astype(BF16), (((0,), (0,)), ((), ())),
                           preferred_element_type=F32)


def _exact_left_dot(mat01, x, terms):
    n = x.shape[1]
    pieces = []
    r = x
    for _ in range(terms):
        p = r.astype(BF16)
        pieces.append(p)
        r = r - p.astype(F32)
    y = _bdot(mat01, jnp.concatenate(pieces, axis=1))
    out = y[:, 0:n]
    for i in range(1, terms):
        out = out + y[:, i * n:(i + 1) * n]
    return out


def _silu(x):
    return x * jax.nn.sigmoid(x)


def _softplus(x):
    return jnp.maximum(x, 0.0) + jnp.log(1.0 + jnp.exp(-jnp.abs(x)))


def _rms_norm_rows(x, w):
    return x * lax.rsqrt(jnp.mean(x * x, axis=-1, keepdims=True) + EPS) * w


def _select_by_level(masks, parts):
    out = jnp.where(masks[-1], parts[-1], 0.0)
    for m, p in zip(masks[-2::-1], parts[-2::-1]):
        out = jnp.where(m, p, out)
    return out


def _in_proj_kernel(n_mix_tiles, x_ref, nw_ref, wm_ref, wg_ref, ws_ref, o_ref, os_ref, xn_ref):
    j = pl.program_id(1)

    @pl.when(j == 0)
    def _():
        xn_ref[...] = _rms_norm_rows(x_ref[...], nw_ref[...]).astype(BF16)
        os_ref[...] = jnp.dot(xn_ref[...], ws_ref[...], preferred_element_type=F32)

    @pl.when(j < n_mix_tiles)
    def _():
        o_ref[...] = jnp.dot(xn_ref[...], wm_ref[...], preferred_element_type=F32)

    @pl.when(j >= n_mix_tiles)
    def _():
        o_ref[...] = jnp.dot(xn_ref[...], wg_ref[...], preferred_element_type=F32)


def _in_proj(x2d, norm_w, w_mix, w_gate, w_small, tm, tn):
    m, d = x2d.shape
    n_mix_tiles = w_mix.shape[1] // tn
    n = w_mix.shape[1] + w_gate.shape[1]
    ns = w_small.shape[1]
    return pl.pallas_call(
        functools.partial(_in_proj_kernel, n_mix_tiles),
        grid=(m // tm, n // tn),
        in_specs=[
            pl.BlockSpec((tm, d), lambda i, j: (i, 0)),
            pl.BlockSpec((1, d), lambda i, j: (0, 0)),
            pl.BlockSpec((d, tn), lambda i, j: (0, jnp.minimum(j, n_mix_tiles - 1))),
            pl.BlockSpec((d, tn), lambda i, j: (0, jnp.maximum(j - n_mix_tiles, 0))),
            pl.BlockSpec((d, ns), lambda i, j: (0, 0)),
        ],
        out_specs=[
            pl.BlockSpec((tm, tn), lambda i, j: (i, j)),
            pl.BlockSpec((tm, ns), lambda i, j: (i, 0)),
        ],
        out_shape=[jax.ShapeDtypeStruct((m, n), F32), jax.ShapeDtypeStruct((m, ns), F32)],
        scratch_shapes=[pltpu.VMEM((tm, d), BF16)],
        compiler_params=pltpu.CompilerParams(
            dimension_semantics=("arbitrary", "arbitrary"), vmem_limit_bytes=VMEM_LIMIT),
        name="in_proj",
    )(x2d, norm_w, w_mix, w_gate, w_small)


def _unit_lower_inverse_minus_eye(a_list, masks):
    e_list = [-jnp.where(masks[len(LEVELS)], a, 0.0) for a in a_list]
    for idx in range(len(LEVELS) - 2, -1, -1):
        off = [jnp.where(masks[idx + 1], a, 0.0) for a in a_list]
        x_list = [o + _dot(o, e) for o, e in zip(off, e_list)]
        e_list = [e - x - _dot(e, x) for e, x in zip(e_list, x_list)]
    return e_list


def _hgrn2_intra(p_ref, rows, lb_all, row_valid, cum_mat, masks):
    tasks = []
    for r0, valid in zip(rows, row_valid):
        for h in range(HEADS):
            lo, hi = h * HEAD_DIM, (h + 1) * HEAD_DIM
            lb = lb_all[:, lo:hi]
            q = _silu(p_ref[0, r0:r0 + CHUNK, lo:hi])
            f_raw = p_ref[0, r0:r0 + CHUNK, MIX_W + lo:MIX_W + hi]
            f = lb + (1.0 - lb) * jax.nn.sigmoid(f_raw)
            k = (1.0 - lb) * jax.nn.sigmoid(-f_raw)
            log_f = jnp.log(f) * LOG2_E
            if valid is not None:
                k = jnp.where(valid, k, 0.0)
                log_f = jnp.where(valid, log_f, 0.0)
                f = jnp.where(valid, f, 1.0)
            v = p_ref[0, r0:r0 + CHUNK, 2 * MIX_W + lo:2 * MIX_W + hi].astype(BF16)
            tasks.append(dict(q=q, f=f, k=k, log_f=log_f, v=v))
    for t in tasks:
        t["b"] = _exact_left_dot(cum_mat, t.pop("log_f"), terms=2)
    nt = lambda x, y: lax.dot_general(x, y, (((1,), (1,)), ((), ())), preferred_element_type=F32)
    for t in tasks:
        q, k, b = t["q"], t["k"], t.pop("b")
        e_cum = jnp.exp2(b)
        e_rest = jnp.exp2(jnp.broadcast_to(b[CHUNK - 1:CHUNK, :], b.shape) - b)
        qb, kb = q.astype(BF16), k.astype(BF16)
        parts = [nt(qb, kb)]
        for m in LEVELS:
            if m == 1:
                parts.append(nt(qb * t["f"].astype(BF16), kb))
            else:
                e_l = jnp.exp2(-jnp.abs(b - _block_midpoint_rows(b, m))).astype(BF16)
                parts.append(nt(qb * e_l, kb * e_l))
        t["scores"] = _select_by_level(masks, parts).astype(BF16)
        t["q_cum"] = (q * e_cum).astype(BF16)
        t["k_rest"] = (k * e_rest).astype(BF16)
        t["e_last"] = e_cum[CHUNK - 1:CHUNK, :]
        del t["q"], t["k"], t["f"]
    return tasks


def _block_midpoint_rows(b, m):
    if 2 * m >= SUBLANES:
        pieces = [jnp.broadcast_to(b[r0 + m - 1:r0 + m, :], (2 * m, b.shape[1])) for r0 in range(0, CHUNK, 2 * m)]
        return jnp.concatenate(pieces, axis=0)
    assert 4 * m == SUBLANES
    lower = lax.broadcasted_iota(jnp.int32, (SUBLANES, b.shape[1]), 0) < 2 * m
    pieces = []
    for r0 in range(0, CHUNK, SUBLANES):
        lo_half = jnp.broadcast_to(b[r0 + m - 1:r0 + m, :], (SUBLANES, b.shape[1]))
        hi_half = jnp.broadcast_to(b[r0 + 3 * m - 1:r0 + 3 * m, :], (SUBLANES, b.shape[1]))
        pieces.append(jnp.where(lower, lo_half, hi_half))
    return jnp.concatenate(pieces, axis=0)


def _gdn_intra(conv, rows, beta_all, gc_all, gc_rows, egc_all, erest_all, masks):
    strict = masks[1]
    for m in masks[2:]:
        strict = strict | m
    incl = strict | masks[0]
    tasks = []
    for si, r0 in enumerate(rows):
        for h in range(HEADS):
            lo, hi = h * HEAD_DIM, (h + 1) * HEAD_DIM
            q = _silu(conv[h][r0:r0 + CHUNK])
            k = _silu(conv[HEADS + h][r0:r0 + CHUNK])
            v = _silu(conv[2 * HEADS + h][r0:r0 + CHUNK])
            q = q * (lax.rsqrt(jnp.sum(q * q, axis=-1, keepdims=True) + EPS) * (HEAD_DIM ** -0.5))
            k = k * lax.rsqrt(jnp.sum(k * k, axis=-1, keepdims=True) + EPS)
            beta_c = beta_all[r0:r0 + CHUNK, h:h + 1]
            gc_c = gc_all[si][:, h:h + 1]
            egc_c = egc_all[si][:, h:h + 1]
            tasks.append(dict(
                q=q, k=k, beta_c=beta_c,
                rel=jnp.exp(jnp.minimum(gc_c - gc_rows[si][h:h + 1, :], 0.0)),
                rhs=jnp.concatenate([(beta_c * egc_c) * k, beta_c * v], axis=1),
                q_cum=(q * egc_c).astype(BF16),
                k_rest=(k * erest_all[si][:, h:h + 1]).astype(BF16),
                e_last=egc_all[si][CHUNK - 1:CHUNK, h:h + 1]))
    for t in tasks:
        qk_kk = _dot_nt(jnp.concatenate([t.pop("q"), t["k"]], axis=0), t.pop("k"))
        rel = t.pop("rel")
        t["attn"] = jnp.where(incl, qk_kk[0:CHUNK] * rel, 0.0).astype(BF16)
        t["a"] = jnp.where(strict, t.pop("beta_c") * qk_kk[CHUNK:2 * CHUNK] * rel, 0.0)
    e_list = _unit_lower_inverse_minus_eye([t.pop("a") for t in tasks], masks)
    for t, e in zip(tasks, e_list):
        rhs = t.pop("rhs")
        wu = rhs + _dot(e, rhs)
        t["w"] = wu[:, 0:HEAD_DIM].astype(BF16)
        t["u"] = wu[:, HEAD_DIM:2 * HEAD_DIM]
    return tasks


def _recurrence_kernel(n_pad, emit_state, cps, *refs):
    (p_ref, sm_ref, lbl_ref, hgw_ref, cw_ref, alog_ref, dtb_ref, gdw_ref,
     cum_ref, lvl_ref, shg0_ref, sgd0_ref, tail0_ref) = refs[:13]
    if emit_state:
        oa_ref, ob_ref, shg_out, sgd_out, tail_out, shg_ref, sgd_ref, cbuf_ref = refs[13:]
    else:
        oa_ref, ob_ref, shg_ref, sgd_ref, cbuf_ref = refs[13:]
    c = pl.program_id(1)
    n_rows = cps * CHUNK
    rows = [s * CHUNK for s in range(cps)]

    @pl.when(c == 0)
    def _():
        shg_ref[...] = shg0_ref[...]
        sgd_ref[...] = sgd0_ref[...]
        for blk in range(3 * HEADS):
            cbuf_ref[blk, 0:CONV_TAIL, :] = tail0_ref[:, blk * LANES:(blk + 1) * LANES]

    lvl = lvl_ref[...]
    masks = [lvl == i for i in range(len(LEVELS) + 1)]
    cum_mat = cum_ref[...]
    if n_pad:
        row_ids = lax.broadcasted_iota(jnp.int32, (n_rows, 1), 0)
        valid_all = row_ids >= n_pad
        row_valid = [valid_all[r0:r0 + CHUNK] for r0 in rows]
    else:
        valid_all = None
        row_valid = [None] * cps

    ll = lbl_ref[...]
    ee = jnp.exp(ll - jnp.max(ll, axis=0, keepdims=True))
    lb_all = ee[0:1, :] / jnp.sum(ee, axis=0, keepdims=True)

    hg = _hgrn2_intra(p_ref, rows, lb_all, row_valid, cum_mat[0:CHUNK], masks)

    gd0 = 4 * MIX_W
    cw = cw_ref[...]
    first = CONV_TAIL - (CONV_K - 1)
    conv = []
    for blk in range(3 * HEADS):
        lo, hi = blk * LANES, (blk + 1) * LANES
        cbuf_ref[blk, CONV_TAIL:CONV_TAIL + n_rows, :] = p_ref[0, :, gd0 + lo:gd0 + hi]
        acc = cw[0:1, lo:hi] * cbuf_ref[blk, first:first + n_rows, :]
        for tap in range(1, CONV_K):
            acc = acc + cw[tap:tap + 1, lo:hi] * cbuf_ref[blk, first + tap:first + tap + n_rows, :]
        conv.append(acc)
        cbuf_ref[blk, 0:CONV_TAIL, :] = cbuf_ref[blk, n_rows:n_rows + CONV_TAIL, :]

    beta_all = jax.nn.sigmoid(sm_ref[0, :, LANES:2 * LANES])
    g_all = -jnp.exp(alog_ref[...]) * _softplus(sm_ref[0, :, 0:LANES] + dtb_ref[...])
    if valid_all is not None:
        beta_all = jnp.where(valid_all, beta_all, 0.0)
        g_all = jnp.where(valid_all, g_all, 0.0)
    gsum = [_exact_left_dot(cum_mat, g_all[r0:r0 + CHUNK], terms=3) for r0 in rows]
    gc_all = [g[0:CHUNK] for g in gsum]
    gc_rows = [g.T for g in gc_all]
    egc_all = [jnp.exp(g) for g in gc_all]
    erest_all = [jnp.exp(g[CHUNK:2 * CHUNK]) for g in gsum]

    gd = _gdn_intra(conv, rows, beta_all, gc_all, gc_rows, egc_all, erest_all, masks)

    s_hg = [shg_ref[h] for h in range(HEADS)]
    s_gd = [sgd_ref[h] for h in range(HEADS)]
    hgw = hgw_ref[...]
    gdw = gdw_ref[...]
    for si, r0 in enumerate(rows):
        hg_t = hg[si * HEADS:(si + 1) * HEADS]
        gd_t = gd[si * HEADS:(si + 1) * HEADS]
        o_hg = [lax.dot_general(t["q_cum"], s.astype(BF16), (((1,), (1,)), ((), ())),
                                preferred_element_type=F32) + _bdot(t["scores"], t["v"])
                for t, s in zip(hg_t, s_hg)]
        s_hg = [s * t["e_last"] + _dot_tn(t["v"], t["k_rest"]) for t, s in zip(hg_t, s_hg)]
        wq_s = [_bdot(jnp.concatenate([t["w"], t["q_cum"]], axis=0), s.astype(BF16))
                for t, s in zip(gd_t, s_gd)]
        v_new = [(t["u"] - ws[0:CHUNK]).astype(BF16) for t, ws in zip(gd_t, wq_s)]
        o_gd = [ws[CHUNK:2 * CHUNK] + _bdot(t["attn"], vn) for t, ws, vn in zip(gd_t, wq_s, v_new)]
        s_gd = [s * t["e_last"] + _dot_tn(t["k_rest"], vn) for t, s, vn in zip(gd_t, s_gd, v_new)]
        for h in range(HEADS):
            lo, hi = h * HEAD_DIM, (h + 1) * HEAD_DIM
            gate = p_ref[0, r0:r0 + CHUNK, 3 * MIX_W + lo:3 * MIX_W + hi]
            oa_ref[0, r0:r0 + CHUNK, lo:hi] = (_rms_norm_rows(o_hg[h], hgw) * _silu(gate)).astype(oa_ref.dtype)
            gate = p_ref[0, r0:r0 + CHUNK, gd0 + 3 * MIX_W + lo:gd0 + 3 * MIX_W + hi]
            ob_ref[0, r0:r0 + CHUNK, lo:hi] = (_rms_norm_rows(o_gd[h], gdw) * _silu(gate)).astype(ob_ref.dtype)
    for h in range(HEADS):
        shg_ref[h] = s_hg[h]
        sgd_ref[h] = s_gd[h]

    if emit_state:
        @pl.when(c == pl.num_programs(1) - 1)
        def _():
            shg_out[...] = shg_ref[...]
            sgd_out[...] = sgd_ref[...]
            for blk in range(3 * HEADS):
                tail_out[:, blk * LANES:(blk + 1) * LANES] = cbuf_ref[blk, 0:CONV_TAIL, :]


def _recurrence(proj, small, lb_logits, hg_norm_w, conv_w, a_log, dt_bias, gd_norm_w,
                shg0, sgd0, tail0, n_pad, emit_state, cps):
    bsz, length, _ = proj.shape
    n_rows = cps * CHUNK
    cum = jnp.asarray(_cumsum_matrix(), BF16)
    lvl = jnp.asarray(_pair_level_table())
    const = lambda *shape: pl.BlockSpec(shape, lambda b, c: (0,) * len(shape))
    state_shape = (HEADS, HEAD_DIM, HEAD_DIM)
    in_specs = [
        pl.BlockSpec((1, n_rows, 8 * MIX_W), lambda b, c: (b, c, 0)),
        pl.BlockSpec((1, n_rows, 2 * LANES), lambda b, c: (b, c, 0)),
        const(*lb_logits.shape), const(1, HEAD_DIM), const(CONV_K, 3 * MIX_W),
        const(1, LANES), const(1, LANES), const(1, HEAD_DIM),
        const(*cum.shape), const(CHUNK, CHUNK),
        const(*state_shape), const(*state_shape), const(CONV_TAIL, 3 * MIX_W),
    ]
    out_specs = [
        pl.BlockSpec((1, n_rows, MIX_W), lambda b, c: (b, c, 0)),
        pl.BlockSpec((1, n_rows, MIX_W), lambda b, c: (b, c, 0)),
    ]
    out_shape = [jax.ShapeDtypeStruct((bsz, length, MIX_W), BF16),
                 jax.ShapeDtypeStruct((bsz, length, MIX_W), BF16)]
    if emit_state:
        out_specs += [const(*state_shape), const(*state_shape), const(CONV_TAIL, 3 * MIX_W)]
        out_shape += [jax.ShapeDtypeStruct(state_shape, F32), jax.ShapeDtypeStruct(state_shape, F32),
                      jax.ShapeDtypeStruct((CONV_TAIL, 3 * MIX_W), F32)]
    return pl.pallas_call(
        functools.partial(_recurrence_kernel, n_pad, emit_state, cps),
        grid=(bsz, length // n_rows),
        in_specs=in_specs,
        out_specs=out_specs,
        out_shape=out_shape,
        scratch_shapes=[pltpu.VMEM(state_shape, F32), pltpu.VMEM(state_shape, F32),
                        pltpu.VMEM((3 * HEADS, CONV_TAIL + n_rows, LANES), F32)],
        compiler_params=pltpu.CompilerParams(
            dimension_semantics=("arbitrary", "arbitrary"), vmem_limit_bytes=VMEM_LIMIT),
        name="recurrence_meta" if emit_state else "recurrence",
    )(proj, small, lb_logits, hg_norm_w, conv_w, a_log, dt_bias, gd_norm_w,
      cum, lvl, shg0, sgd0, tail0)


def _merge_kernel(oa_ref, ob_ref, g_ref, x_ref, wa_ref, wb_ref, wo_ref, nw_ref, h_ref, xn_ref):
    d = x_ref.shape[1]
    ya = jnp.dot(oa_ref[...], wa_ref[...], preferred_element_type=F32)
    yb = jnp.dot(ob_ref[...], wb_ref[...], preferred_element_type=F32)
    merged = jax.nn.sigmoid(g_ref[:, 0:d]) * ya + jax.nn.sigmoid(g_ref[:, d:2 * d]) * yb
    h = x_ref[...] + jnp.dot(merged.astype(BF16), wo_ref[...], preferred_element_type=F32)
    h_ref[...] = h
    xn_ref[...] = _rms_norm_rows(h, nw_ref[...]).astype(BF16)


def _merge(oa, ob, proj, gate_block, x2d, wa, wb, wo, ffn_norm_w, tm):
    m, d = x2d.shape
    single = dict(pipeline_mode=pl.Buffered(1))
    return pl.pallas_call(
        _merge_kernel,
        grid=(m // tm,),
        in_specs=[
            pl.BlockSpec((tm, MIX_W), lambda i: (i, 0)),
            pl.BlockSpec((tm, MIX_W), lambda i: (i, 0)),
            pl.BlockSpec((tm, 2 * d), lambda i: (i, gate_block)),
            pl.BlockSpec((tm, d), lambda i: (i, 0)),
            pl.BlockSpec((MIX_W, d), lambda i: (0, 0), **single),
            pl.BlockSpec((MIX_W, d), lambda i: (0, 0), **single),
            pl.BlockSpec((d, d), lambda i: (0, 0), **single),
            pl.BlockSpec((1, d), lambda i: (0, 0)),
        ],
        out_specs=[pl.BlockSpec((tm, d), lambda i: (i, 0)), pl.BlockSpec((tm, d), lambda i: (i, 0))],
        out_shape=[jax.ShapeDtypeStruct((m, d), F32), jax.ShapeDtypeStruct((m, d), BF16)],
        compiler_params=pltpu.CompilerParams(
            dimension_semantics=("arbitrary",), vmem_limit_bytes=VMEM_LIMIT),
        name="merge",
    )(oa, ob, proj, x2d, wa, wb, wo, ffn_norm_w)


def _ffn_kernel(xn_ref, h_ref, wg_ref, wu_ref, wd_ref, nw_ref, o_ref, hid_ref):
    j = pl.program_id(1)
    nf = pl.num_programs(1) - 1

    def gate_up(slot):
        xn = xn_ref[...]
        gate = jnp.dot(xn, wg_ref[...], preferred_element_type=F32)
        up = jnp.dot(xn, wu_ref[...], preferred_element_type=F32)
        hid_ref[slot] = (_silu(gate) * up).astype(BF16)

    def down(slot):
        o_ref[...] += jnp.dot(hid_ref[slot], wd_ref[...], preferred_element_type=F32)

    @pl.when(j == 0)
    def _():
        o_ref[...] = h_ref[...]
        gate_up(0)

    @pl.when((j > 0) & (j < nf))
    def _():
        gate_up(j % 2)
        down((j - 1) % 2)

    @pl.when(j == nf)
    def _():
        down((j - 1) % 2)
        o_ref[...] = _rms_norm_rows(o_ref[...], nw_ref[...])


def _ffn(xn, h, w_gate_up, w_down, final_norm_w, tm, tf):
    m, d = h.shape
    ff = w_down.shape[0]
    nf = ff // tf
    return pl.pallas_call(
        _ffn_kernel,
        grid=(m // tm, nf + 1),
        in_specs=[
            pl.BlockSpec((tm, d), lambda i, j: (i, 0)),
            pl.BlockSpec((tm, d), lambda i, j: (i, 0)),
            pl.BlockSpec((d, tf), lambda i, j: (0, jnp.minimum(j, nf - 1))),
            pl.BlockSpec((d, tf), lambda i, j: (0, nf + jnp.minimum(j, nf - 1))),
            pl.BlockSpec((tf, d), lambda i, j: (jnp.maximum(j - 1, 0), 0)),
            pl.BlockSpec((1, d), lambda i, j: (0, 0)),
        ],
        out_specs=pl.BlockSpec((tm, d), lambda i, j: (i, 0)),
        out_shape=jax.ShapeDtypeStruct((m, d), F32),
        scratch_shapes=[pltpu.VMEM((2, tm, tf), BF16)],
        compiler_params=pltpu.CompilerParams(
            dimension_semantics=("arbitrary", "arbitrary"), vmem_limit_bytes=VMEM_LIMIT),
        name="ffn",
    )(xn, h, w_gate_up, w_gate_up, w_down, final_norm_w)


def _pad_lanes(v):
    return jnp.pad(v.astype(F32), (0, LANES - v.shape[0]))[None, :]


def kernel(x, meta_tokens, lb_logits, mix_norm_w, w_in, hg_norm_w, gd_conv_w, gd_a_log, gd_dt_bias,
           gd_norm_w, w_branch_a, w_branch_b, w_out, ffn_norm_w, w_ffn_in, w_ffn_out, final_norm_w):
    bsz, seq, d = x.shape
    layer = 0
    assert mix_norm_w.shape[0] == 1, "single-layer block"
    assert seq % (2 * CHUNK) == 0 and meta_tokens.shape[0] == N_META <= CHUNK

    w = w_in[layer]
    n_mix = 8 * MIX_W
    w_mix = w[:, :n_mix].astype(BF16)
    w_gate = w[:, n_mix + 2 * HEADS:].astype(BF16)
    pad = jnp.zeros((d, LANES - HEADS), w.dtype)
    w_small = jnp.concatenate([w[:, n_mix:n_mix + HEADS], pad,
                               w[:, n_mix + HEADS:n_mix + 2 * HEADS], pad], axis=1).astype(BF16)
    norm_w = mix_norm_w[layer][None, :]
    rec_params = (lb_logits.astype(F32), hg_norm_w[layer][None, :], gd_conv_w[layer],
                  _pad_lanes(gd_a_log[layer]), _pad_lanes(gd_dt_bias[layer]), gd_norm_w[layer][None, :])

    n_pad = CHUNK - N_META
    x_meta = jnp.concatenate([jnp.zeros((n_pad, d), x.dtype), meta_tokens.astype(x.dtype)], axis=0)
    proj_m, small_m = _in_proj(x_meta, norm_w, w_mix, w_gate, w_small, tm=CHUNK, tn=1024)
    zero_state = jnp.zeros((HEADS, HEAD_DIM, HEAD_DIM), F32)
    zero_tail = jnp.zeros((CONV_TAIL, 3 * MIX_W), F32)
    _, _, shg0, sgd0, tail0 = _recurrence(proj_m[None], small_m[None], *rec_params,
                                          zero_state, zero_state, zero_tail,
                                          n_pad=n_pad, emit_state=True, cps=1)

    x2d = x.reshape(bsz * seq, d)
    proj, small = _in_proj(x2d, norm_w, w_mix, w_gate, w_small, tm=min(1024, bsz * seq), tn=1024)
    oa, ob = _recurrence(proj.reshape(bsz, seq, -1), small.reshape(bsz, seq, -1), *rec_params,
                         shg0, sgd0, tail0, n_pad=0, emit_state=False, cps=2)

    h1, xn2 = _merge(oa.reshape(bsz * seq, MIX_W), ob.reshape(bsz * seq, MIX_W), proj,
                     n_mix // (2 * d), x2d, w_branch_a[layer].astype(BF16), w_branch_b[layer].astype(BF16),
                     w_out[layer].astype(BF16), ffn_norm_w[layer][None, :], tm=min(256, bsz * seq))
    out = _ffn(xn2, h1, w_ffn_in[layer].astype(BF16), w_ffn_out[layer].astype(BF16),
               final_norm_w[None, :], tm=min(512, bsz * seq), tf=512)
    return out.reshape(bsz, seq, d)
```

```python
import functools

import numpy as np
import jax
import jax.numpy as jnp
from jax import lax
from jax.experimental import pallas as pl
from jax.experimental.pallas import tpu as pltpu

F32 = jnp.float32
BF16 = jnp.bfloat16

EPS = 1e-6
LOG2_E = 1.4426950408889634
N_META = 16
CHUNK = 64
HEADS = 8
HEAD_DIM = 128
MIX_W = HEADS * HEAD_DIM
REC_CHUNKS_PER_STEP = 4
CONV_K = 4
CONV_TAIL = 8
LANES = 128
SUBLANES = 8
VMEM_LIMIT = 56 * 1024 * 1024
VMEM_LIMIT_FFN = 60 * 1024 * 1024

LEVELS = tuple(CHUNK >> (i + 1) for i in range(CHUNK.bit_length() - 1))


def _pair_level_table():
    t = np.arange(CHUNK)[:, None]
    j = np.arange(CHUNK)[None, :]
    lvl = np.where(j > t, -1, 0).astype(np.int32)
    for idx, m in enumerate(LEVELS):
        pair = (t // (2 * m) == j // (2 * m)) & (t // m != j // m) & (j < t)
        lvl = np.where(pair, idx + 1, lvl)
    return lvl


def _cumsum_matrix():
    t = np.arange(CHUNK)[:, None]
    i = np.arange(CHUNK)[None, :]
    return np.concatenate([i <= t, i > t], axis=0).astype(np.float32)


def _bdot(a, b):
    return jnp.dot(a, b, preferred_element_type=F32)


def _dot(a, b):
    return _bdot(a.astype(BF16), b.astype(BF16))


def _dot_nt(a, b):
    return lax.dot_general(a.astype(BF16), b.astype(BF16), (((1,), (1,)), ((), ())),
                           preferred_element_type=F32)


def _dot_tn(a, b):
    return lax.dot_general(a.astype(BF16), b.astype(BF16), (((0,), (0,)), ((), ())),
                           preferred_element_type=F32)


def _exact_left_dot(mat01, x, terms):
    n = x.shape[1]
    pieces = []
    r = x
    for _ in range(terms):
        p = r.astype(BF16)
        pieces.append(p)
        r = r - p.astype(F32)
    y = _bdot(mat01, jnp.concatenate(pieces, axis=1))
    out = y[:, 0:n]
    for i in range(1, terms):
        out = out + y[:, i * n:(i + 1) * n]
    return out


def _silu(x):
    return x * jax.nn.sigmoid(x)


def _softplus(x):
    return jnp.maximum(x, 0.0) + jnp.log(1.0 + jnp.exp(-jnp.abs(x)))


def _rms_norm_rows(x, w):
    return x * lax.rsqrt(jnp.mean(x * x, axis=-1, keepdims=True) + EPS) * w


def _select_by_level(masks, parts):
    out = jnp.where(masks[-1], parts[-1], 0.0)
    for m, p in zip(masks[-2::-1], parts[-2::-1]):
        out = jnp.where(m, p, out)
    return out


def _in_proj_kernel(n_mix_tiles, x_ref, nw_ref, wm_ref, wg_ref, ws_ref, o_ref, os_ref, xn_ref):
    j = pl.program_id(1)

    @pl.when(j == 0)
    def _():
        xn_ref[...] = _rms_norm_rows(x_ref[...], nw_ref[...]).astype(BF16)
        os_ref[...] = jnp.dot(xn_ref[...], ws_ref[...], preferred_element_type=F32)

    @pl.when(j < n_mix_tiles)
    def _():
        o_ref[...] = jnp.dot(xn_ref[...], wm_ref[...], preferred_element_type=F32)

    @pl.when(j >= n_mix_tiles)
    def _():
        o_ref[...] = jnp.dot(xn_ref[...], wg_ref[...], preferred_element_type=F32)


def _in_proj(x2d, norm_w, w_mix, w_gate, w_small, tm, tn):
    m, d = x2d.shape
    n_mix_tiles = w_mix.shape[1] // tn
    n = w_mix.shape[1] + w_gate.shape[1]
    ns = w_small.shape[1]
    return pl.pallas_call(
        functools.partial(_in_proj_kernel, n_mix_tiles),
        grid=(m // tm, n // tn),
        in_specs=[
            pl.BlockSpec((tm, d), lambda i, j: (i, 0)),
            pl.BlockSpec((1, d), lambda i, j: (0, 0)),
            pl.BlockSpec((d, tn), lambda i, j: (0, jnp.minimum(j, n_mix_tiles - 1))),
            pl.BlockSpec((d, tn), lambda i, j: (0, jnp.maximum(j - n_mix_tiles, 0))),
            pl.BlockSpec((d, ns), lambda i, j: (0, 0)),
        ],
        out_specs=[
            pl.BlockSpec((tm, tn), lambda i, j: (i, j)),
            pl.BlockSpec((tm, ns), lambda i, j: (i, 0)),
        ],
        out_shape=[jax.ShapeDtypeStruct((m, n), F32), jax.ShapeDtypeStruct((m, ns), F32)],
        scratch_shapes=[pltpu.VMEM((tm, d), BF16)],
        compiler_params=pltpu.CompilerParams(
            dimension_semantics=("arbitrary", "arbitrary"), vmem_limit_bytes=VMEM_LIMIT),
        name="in_proj",
    )(x2d, norm_w, w_mix, w_gate, w_small)


def _unit_lower_inverse_minus_eye(a_list, masks):
    e_list = [-jnp.where(masks[len(LEVELS)], a, 0.0) for a in a_list]
    for idx in range(len(LEVELS) - 2, -1, -1):
        off = [jnp.where(masks[idx + 1], a, 0.0) for a in a_list]
        x_list = [o + _dot(o, e) for o, e in zip(off, e_list)]
        e_list = [e - x - _dot(e, x) for e, x in zip(e_list, x_list)]
    return e_list


def _hgrn2_intra(p_ref, rows, lb_all, row_valid, cum_mat, masks):
    tasks = []
    for r0, valid in zip(rows, row_valid):
        for h in range(HEADS):
            lo, hi = h * HEAD_DIM, (h + 1) * HEAD_DIM
            lb = lb_all[:, lo:hi]
            q = _silu(p_ref[0, r0:r0 + CHUNK, lo:hi])
            f_raw = p_ref[0, r0:r0 + CHUNK, MIX_W + lo:MIX_W + hi]
            f = lb + (1.0 - lb) * jax.nn.sigmoid(f_raw)
            k = (1.0 - lb) * jax.nn.sigmoid(-f_raw)
            log_f = jnp.log(f) * LOG2_E
            if valid is not None:
                k = jnp.where(valid, k, 0.0)
                log_f = jnp.where(valid, log_f, 0.0)
                f = jnp.where(valid, f, 1.0)
            v = p_ref[0, r0:r0 + CHUNK, 2 * MIX_W + lo:2 * MIX_W + hi].astype(BF16)
            tasks.append(dict(q=q, f=f, k=k, log_f=log_f, v=v))
    for t in tasks:
        t["b"] = _exact_left_dot(cum_mat, t.pop("log_f"), terms=2)
    nt = lambda x, y: lax.dot_general(x, y, (((1,), (1,)), ((), ())), preferred_element_type=F32)
    for t in tasks:
        q, k, b = t["q"], t["k"], t.pop("b")
        e_cum = jnp.exp2(b)
        e_rest = jnp.exp2(jnp.broadcast_to(b[CHUNK - 1:CHUNK, :], b.shape) - b)
        qb, kb = q.astype(BF16), k.astype(BF16)
        parts = [nt(qb, kb)]
        for m in LEVELS:
            if m == 1:
                parts.append(nt(qb * t["f"].astype(BF16), kb))
            else:
                e_l = jnp.exp2(-jnp.abs(b - _block_midpoint_rows(b, m))).astype(BF16)
                parts.append(nt(qb * e_l, kb * e_l))
        t["scores"] = _select_by_level(masks, parts).astype(BF16)
        t["q_cum"] = (q * e_cum).astype(BF16)
        t["k_rest"] = (k * e_rest).astype(BF16)
        t["e_last"] = e_cum[CHUNK - 1:CHUNK, :]
        del t["q"], t["k"], t["f"]
    return tasks


def _block_midpoint_rows(b, m):
    if 2 * m >= SUBLANES:
        pieces = [jnp.broadcast_to(b[r0 + m - 1:r0 + m, :], (2 * m, b.shape[1])) for r0 in range(0, CHUNK, 2 * m)]
        return jnp.concatenate(pieces, axis=0)
    assert 4 * m == SUBLANES
    lower = lax.broadcasted_iota(jnp.int32, (SUBLANES, b.shape[1]), 0) < 2 * m
    pieces = []
    for r0 in range(0, CHUNK, SUBLANES):
        lo_half = jnp.broadcast_to(b[r0 + m - 1:r0 + m, :], (SUBLANES, b.shape[1]))
        hi_half = jnp.broadcast_to(b[r0 + 3 * m - 1:r0 + 3 * m, :], (SUBLANES, b.shape[1]))
        pieces.append(jnp.where(lower, lo_half, hi_half))
    return jnp.concatenate(pieces, axis=0)


def _gdn_intra(conv, rows, beta_all, gc_all, gc_rows, egc_all, erest_all, masks):
    strict = masks[1]
    for m in masks[2:]:
        strict = strict | m
    incl = strict | masks[0]
    tasks = []
    for si, r0 in enumerate(rows):
        for h in range(HEADS):
            lo, hi = h * HEAD_DIM, (h + 1) * HEAD_DIM
            q = _silu(conv[h][r0:r0 + CHUNK])
            k = _silu(conv[HEADS + h][r0:r0 + CHUNK])
            v = _silu(conv[2 * HEADS + h][r0:r0 + CHUNK])
            q = q * (lax.rsqrt(jnp.sum(q * q, axis=-1, keepdims=True) + EPS) * (HEAD_DIM ** -0.5))
            k = k * lax.rsqrt(jnp.sum(k * k, axis=-1, keepdims=True) + EPS)
            beta_c = beta_all[r0:r0 + CHUNK, h:h + 1]
            gc_c = gc_all[si][:, h:h + 1]
            egc_c = egc_all[si][:, h:h + 1]
            tasks.append(dict(
                q=q, k=k, beta_c=beta_c,
                rel=jnp.exp(jnp.minimum(gc_c - gc_rows[si][h:h + 1, :], 0.0)),
                rhs=jnp.concatenate([(beta_c * egc_c) * k, beta_c * v], axis=1),
                q_cum=(q * egc_c).astype(BF16),
                k_rest=(k * erest_all[si][:, h:h + 1]).astype(BF16),
                e_last=egc_all[si][CHUNK - 1:CHUNK, h:h + 1]))
    for t in tasks:
        qk_kk = _dot_nt(jnp.concatenate([t.pop("q"), t["k"]], axis=0), t.pop("k"))
        rel = t.pop("rel")
        t["attn"] = jnp.where(incl, qk_kk[0:CHUNK] * rel, 0.0).astype(BF16)
        t["a"] = jnp.where(strict, t.pop("beta_c") * qk_kk[CHUNK:2 * CHUNK] * rel, 0.0)
    e_list = _unit_lower_inverse_minus_eye([t.pop("a") for t in tasks], masks)
    for t, e in zip(tasks, e_list):
        rhs = t.pop("rhs")
        wu = rhs + _dot(e, rhs)
        t["w"] = wu[:, 0:HEAD_DIM].astype(BF16)
        t["u"] = wu[:, HEAD_DIM:2 * HEAD_DIM]
    return tasks


def _recurrence_kernel(n_pad, emit_state, cps, *refs):
    (p_ref, sm_ref, lbl_ref, hgw_ref, cw_ref, alog_ref, dtb_ref, gdw_ref,
     cum_ref, lvl_ref, shg0_ref, sgd0_ref, tail0_ref) = refs[:13]
    if emit_state:
        oa_ref, ob_ref, shg_out, sgd_out, tail_out, shg_ref, sgd_ref, cbuf_ref = refs[13:]
    else:
        oa_ref, ob_ref, shg_ref, sgd_ref, cbuf_ref = refs[13:]
    c = pl.program_id(1)
    n_rows = cps * CHUNK
    rows = [s * CHUNK for s in range(cps)]

    @pl.when(c == 0)
    def _():
        shg_ref[...] = shg0_ref[...]
        sgd_ref[...] = sgd0_ref[...]
        for blk in range(3 * HEADS):
            cbuf_ref[blk, 0:CONV_TAIL, :] = tail0_ref[:, blk * LANES:(blk + 1) * LANES]

    lvl = lvl_ref[...]
    masks = [lvl == i for i in range(len(LEVELS) + 1)]
    cum_mat = cum_ref[...]
    if n_pad:
        row_ids = lax.broadcasted_iota(jnp.int32, (n_rows, 1), 0)
        valid_all = row_ids >= n_pad
        row_valid = [valid_all[r0:r0 + CHUNK] for r0 in rows]
    else:
        valid_all = None
        row_valid = [None] * cps

    ll = lbl_ref[...]
    ee = jnp.exp(ll - jnp.max(ll, axis=0, keepdims=True))
    lb_all = ee[0:1, :] / jnp.sum(ee, axis=0, keepdims=True)

    hg = _hgrn2_intra(p_ref, rows, lb_all, row_valid, cum_mat[0:CHUNK], masks)

    gd0 = 4 * MIX_W
    cw = cw_ref[...]
    first = CONV_TAIL - (CONV_K - 1)
    conv = []
    for blk in range(3 * HEADS):
        lo, hi = blk * LANES, (blk + 1) * LANES
        cbuf_ref[blk, CONV_TAIL:CONV_TAIL + n_rows, :] = p_ref[0, :, gd0 + lo:gd0 + hi]
        acc = cw[0:1, lo:hi] * cbuf_ref[blk, first:first + n_rows, :]
        for tap in range(1, CONV_K):
            acc = acc + cw[tap:tap + 1, lo:hi] * cbuf_ref[blk, first + tap:first + tap + n_rows, :]
        conv.append(acc)
        cbuf_ref[blk, 0:CONV_TAIL, :] = cbuf_ref[blk, n_rows:n_rows + CONV_TAIL, :]

    beta_all = jax.nn.sigmoid(sm_ref[0, :, LANES:2 * LANES])
    g_all = -jnp.exp(alog_ref[...]) * _softplus(sm_ref[0, :, 0:LANES] + dtb_ref[...])
    if valid_all is not None:
        beta_all = jnp.where(valid_all, beta_all, 0.0)
        g_all = jnp.where(valid_all, g_all, 0.0)
    gsum = [_exact_left_dot(cum_mat, g_all[r0:r0 + CHUNK], terms=3) for r0 in rows]
    gc_all = [g[0:CHUNK] for g in gsum]
    gc_rows = [g.T for g in gc_all]
    egc_all = [jnp.exp(g) for g in gc_all]
    erest_all = [jnp.exp(g[CHUNK:2 * CHUNK]) for g in gsum]

    gd = _gdn_intra(conv, rows, beta_all, gc_all, gc_rows, egc_all, erest_all, masks)

    s_hg = [shg_ref[h] for h in range(HEADS)]
    s_gd = [sgd_ref[h] for h in range(HEADS)]
    hgw = hgw_ref[...]
    gdw = gdw_ref[...]
    for si, r0 in enumerate(rows):
        hg_t = hg[si * HEADS:(si + 1) * HEADS]
        gd_t = gd[si * HEADS:(si + 1) * HEADS]
        o_hg = [lax.dot_general(t["q_cum"], s.astype(BF16), (((1,), (1,)), ((), ())),
                                preferred_element_type=F32) + _bdot(t["scores"], t["v"])
                for t, s in zip(hg_t, s_hg)]
        s_hg = [s * t["e_last"] + _dot_tn(t["v"], t["k_rest"]) for t, s in zip(hg_t, s_hg)]
        wq_s = [_bdot(jnp.concatenate([t["w"], t["q_cum"]], axis=0), s.astype(BF16))
                for t, s in zip(gd_t, s_gd)]
        v_new = [(t["u"] - ws[0:CHUNK]).astype(BF16) for t, ws in zip(gd_t, wq_s)]
        o_gd = [ws[CHUNK:2 * CHUNK] + _bdot(t["attn"], vn) for t, ws, vn in zip(gd_t, wq_s, v_new)]
        s_gd = [s * t["e_last"] + _dot_tn(t["k_rest"], vn) for t, s, vn in zip(gd_t, s_gd, v_new)]
        for h in range(HEADS):
            lo, hi = h * HEAD_DIM, (h + 1) * HEAD_DIM
            gate = p_ref[0, r0:r0 + CHUNK, 3 * MIX_W + lo:3 * MIX_W + hi]
            oa_ref[0, r0:r0 + CHUNK, lo:hi] = (_rms_norm_rows(o_hg[h], hgw) * _silu(gate)).astype(oa_ref.dtype)
            gate = p_ref[0, r0:r0 + CHUNK, gd0 + 3 * MIX_W + lo:gd0 + 3 * MIX_W + hi]
            ob_ref[0, r0:r0 + CHUNK, lo:hi] = (_rms_norm_rows(o_gd[h], gdw) * _silu(gate)).astype(ob_ref.dtype)
    for h in range(HEADS):
        shg_ref[h] = s_hg[h]
        sgd_ref[h] = s_gd[h]

    if emit_state:
        @pl.when(c == pl.num_programs(1) - 1)
        def _():
            shg_out[...] = shg_ref[...]
            sgd_out[...] = sgd_ref[...]
            for blk in range(3 * HEADS):
                tail_out[:, blk * LANES:(blk + 1) * LANES] = cbuf_ref[blk, 0:CONV_TAIL, :]


def _recurrence(proj, small, lb_logits, hg_norm_w, conv_w, a_log, dt_bias, gd_norm_w,
                shg0, sgd0, tail0, n_pad, emit_state, cps):
    bsz, length, _ = proj.shape
    n_rows = cps * CHUNK
    cum = jnp.asarray(_cumsum_matrix(), BF16)
    lvl = jnp.asarray(_pair_level_table())
    const = lambda *shape: pl.BlockSpec(shape, lambda b, c: (0,) * len(shape))
    state_shape = (HEADS, HEAD_DIM, HEAD_DIM)
    in_specs = [
        pl.BlockSpec((1, n_rows, 8 * MIX_W), lambda b, c: (b, c, 0)),
        pl.BlockSpec((1, n_rows, 2 * LANES), lambda b, c: (b, c, 0)),
        const(*lb_logits.shape), const(1, HEAD_DIM), const(CONV_K, 3 * MIX_W),
        const(1, LANES), const(1, LANES), const(1, HEAD_DIM),
        const(*cum.shape), const(CHUNK, CHUNK),
        const(*state_shape), const(*state_shape), const(CONV_TAIL, 3 * MIX_W),
    ]
    out_specs = [
        pl.BlockSpec((1, n_rows, MIX_W), lambda b, c: (b, c, 0)),
        pl.BlockSpec((1, n_rows, MIX_W), lambda b, c: (b, c, 0)),
    ]
    out_shape = [jax.ShapeDtypeStruct((bsz, length, MIX_W), BF16),
                 jax.ShapeDtypeStruct((bsz, length, MIX_W), BF16)]
    if emit_state:
        out_specs += [const(*state_shape), const(*state_shape), const(CONV_TAIL, 3 * MIX_W)]
        out_shape += [jax.ShapeDtypeStruct(state_shape, F32), jax.ShapeDtypeStruct(state_shape, F32),
                      jax.ShapeDtypeStruct((CONV_TAIL, 3 * MIX_W), F32)]
    return pl.pallas_call(
        functools.partial(_recurrence_kernel, n_pad, emit_state, cps),
        grid=(bsz, length // n_rows),
        in_specs=in_specs,
        out_specs=out_specs,
        out_shape=out_shape,
        scratch_shapes=[pltpu.VMEM(state_shape, F32), pltpu.VMEM(state_shape, F32),
                        pltpu.VMEM((3 * HEADS, CONV_TAIL + n_rows, LANES), F32)],
        compiler_params=pltpu.CompilerParams(
            dimension_semantics=("arbitrary", "arbitrary"), vmem_limit_bytes=VMEM_LIMIT),
        name="recurrence_meta" if emit_state else "recurrence",
    )(proj, small, lb_logits, hg_norm_w, conv_w, a_log, dt_bias, gd_norm_w,
      cum, lvl, shg0, sgd0, tail0)


def _merge_kernel(oa_ref, ob_ref, g_ref, x_ref, wa_ref, wb_ref, wo_ref, h_ref):
    d = x_ref.shape[1]
    ya = jnp.dot(oa_ref[...], wa_ref[...], preferred_element_type=F32)
    yb = jnp.dot(ob_ref[...], wb_ref[...], preferred_element_type=F32)
    merged = jax.nn.sigmoid(g_ref[:, 0:d]) * ya + jax.nn.sigmoid(g_ref[:, d:2 * d]) * yb
    h_ref[...] = x_ref[...] + jnp.dot(merged.astype(BF16), wo_ref[...], preferred_element_type=F32)


def _merge(oa, ob, proj, gate_block, x2d, wa, wb, wo, tm):
    m, d = x2d.shape
    single = dict(pipeline_mode=pl.Buffered(1))
    return pl.pallas_call(
        _merge_kernel,
        grid=(m // tm,),
        in_specs=[
            pl.BlockSpec((tm, MIX_W), lambda i: (i, 0)),
            pl.BlockSpec((tm, MIX_W), lambda i: (i, 0)),
            pl.BlockSpec((tm, 2 * d), lambda i: (i, gate_block)),
            pl.BlockSpec((tm, d), lambda i: (i, 0)),
            pl.BlockSpec((MIX_W, d), lambda i: (0, 0), **single),
            pl.BlockSpec((MIX_W, d), lambda i: (0, 0), **single),
            pl.BlockSpec((d, d), lambda i: (0, 0), **single),
        ],
        out_specs=pl.BlockSpec((tm, d), lambda i: (i, 0)),
        out_shape=jax.ShapeDtypeStruct((m, d), F32),
        compiler_params=pltpu.CompilerParams(
            dimension_semantics=("arbitrary",), vmem_limit_bytes=VMEM_LIMIT),
        name="merge",
    )(oa, ob, proj, x2d, wa, wb, wo)


def _ffn_kernel(h_ref, wg_ref, wu_ref, wd_ref, nw_ref, fw_ref, o_ref, xn_ref, hid_ref):
    j = pl.program_id(1)
    nf = pl.num_programs(1) - 1

    def gate_up(slot):
        xn = xn_ref[...]
        gate = jnp.dot(xn, wg_ref[...], preferred_element_type=F32)
        up = jnp.dot(xn, wu_ref[...], preferred_element_type=F32)
        hid_ref[slot] = (_silu(gate) * up).astype(BF16)

    def down(slot):
        o_ref[...] += jnp.dot(hid_ref[slot], wd_ref[...], preferred_element_type=F32)

    @pl.when(j == 0)
    def _():
        h = h_ref[...]
        o_ref[...] = h
        xn_ref[...] = _rms_norm_rows(h, nw_ref[...]).astype(BF16)
        gate_up(0)

    @pl.when((j > 0) & (j < nf))
    def _():
        gate_up(j % 2)
        down((j - 1) % 2)

    @pl.when(j == nf)
    def _():
        down((j - 1) % 2)
        o_ref[...] = _rms_norm_rows(o_ref[...], fw_ref[...])


def _ffn(h, w_gate_up, w_down, ffn_norm_w, final_norm_w, tm, tf):
    m, d = h.shape
    ff = w_down.shape[0]
    nf = ff // tf
    return pl.pallas_call(
        _ffn_kernel,
        grid=(m // tm, nf + 1),
        in_specs=[
            pl.BlockSpec((tm, d), lambda i, j: (i, 0)),
            pl.BlockSpec((d, tf), lambda i, j: (0, jnp.minimum(j, nf - 1))),
            pl.BlockSpec((d, tf), lambda i, j: (0, nf + jnp.minimum(j, nf - 1))),
            pl.BlockSpec((tf, d), lambda i, j: (jnp.maximum(j - 1, 0), 0)),
            pl.BlockSpec((1, d), lambda i, j: (0, 0)),
            pl.BlockSpec((1, d), lambda i, j: (0, 0)),
        ],
        out_specs=pl.BlockSpec((tm, d), lambda i, j: (i, 0)),
        out_shape=jax.ShapeDtypeStruct((m, d), F32),
        scratch_shapes=[pltpu.VMEM((tm, d), BF16), pltpu.VMEM((2, tm, tf), BF16)],
        compiler_params=pltpu.CompilerParams(
            dimension_semantics=("arbitrary", "arbitrary"), vmem_limit_bytes=VMEM_LIMIT_FFN),
        name="ffn",
    )(h, w_gate_up, w_gate_up, w_down, ffn_norm_w, final_norm_w)


def _pad_lanes(v):
    return jnp.pad(v.astype(F32), (0, LANES - v.shape[0]))[None, :]


def kernel(x, meta_tokens, lb_logits, mix_norm_w, w_in, hg_norm_w, gd_conv_w, gd_a_log, gd_dt_bias,
           gd_norm_w, w_branch_a, w_branch_b, w_out, ffn_norm_w, w_ffn_in, w_ffn_out, final_norm_w):
    bsz, seq, d = x.shape
    layer = 0
    assert mix_norm_w.shape[0] == 1, "single-layer block"
    assert seq % (REC_CHUNKS_PER_STEP * CHUNK) == 0 and meta_tokens.shape[0] == N_META <= CHUNK

    w = w_in[layer]
    n_mix = 8 * MIX_W
    w_mix = w[:, :n_mix].astype(BF16)
    w_gate = w[:, n_mix + 2 * HEADS:].astype(BF16)
    pad = jnp.zeros((d, LANES - HEADS), w.dtype)
    w_small = jnp.concatenate([w[:, n_mix:n_mix + HEADS], pad,
                               w[:, n_mix + HEADS:n_mix + 2 * HEADS], pad], axis=1).astype(BF16)
    norm_w = mix_norm_w[layer][None, :]
    rec_params = (lb_logits.astype(F32), hg_norm_w[layer][None, :], gd_conv_w[layer],
                  _pad_lanes(gd_a_log[layer]), _pad_lanes(gd_dt_bias[layer]), gd_norm_w[layer][None, :])

    n_pad = CHUNK - N_META
    x_meta = jnp.concatenate([jnp.zeros((n_pad, d), x.dtype), meta_tokens.astype(x.dtype)], axis=0)
    proj_m, small_m = _in_proj(x_meta, norm_w, w_mix, w_gate, w_small, tm=CHUNK, tn=1024)
    zero_state = jnp.zeros((HEADS, HEAD_DIM, HEAD_DIM), F32)
    zero_tail = jnp.zeros((CONV_TAIL, 3 * MIX_W), F32)
    _, _, shg0, sgd0, tail0 = _recurrence(proj_m[None], small_m[None], *rec_params,
                                          zero_state, zero_state, zero_tail,
                                          n_pad=n_pad, emit_state=True, cps=1)

    x2d = x.reshape(bsz * seq, d)
    proj, small = _in_proj(x2d, norm_w, w_mix, w_gate, w_small, tm=min(1024, bsz * seq), tn=1024)
    oa, ob = _recurrence(proj.reshape(bsz, seq, -1), small.reshape(bsz, seq, -1), *rec_params,
                         shg0, sgd0, tail0, n_pad=0, emit_state=False, cps=REC_CHUNKS_PER_STEP)

    h1 = _merge(oa.reshape(bsz * seq, MIX_W), ob.reshape(bsz * seq, MIX_W), proj,
                n_mix // (2 * d), x2d, w_branch_a[layer].astype(BF16), w_branch_b[layer].astype(BF16),
                w_out[layer].astype(BF16), tm=min(256, bsz * seq))
    out = _ffn(h1, w_ffn_in[layer].astype(BF16), w_ffn_out[layer].astype(BF16),
               ffn_norm_w[layer][None, :], final_norm_w[None, :], tm=min(1024, bsz * seq), tf=512)
    return out.reshape(bsz, seq, d)
```

```python
import functools

import numpy as np
import jax
import jax.numpy as jnp
from jax import lax
from jax.experimental import pallas as pl
from jax.experimental.pallas import tpu as pltpu

F32 = jnp.float32
BF16 = jnp.bfloat16

EPS = 1e-6
LOG2_E = 1.4426950408889634
N_META = 16
CHUNK = 64
HEADS = 8
HEAD_DIM = 128
MIX_W = HEADS * HEAD_DIM
REC_CHUNKS_PER_STEP = 4
CONV_K = 4
CONV_TAIL = 8
LANES = 128
SUBLANES = 8
VMEM_LIMIT = 56 * 1024 * 1024
VMEM_LIMIT_FFN = 60 * 1024 * 1024

LEVELS = tuple(CHUNK >> (i + 1) for i in range(CHUNK.bit_length() - 1))


def _pair_level_table():
    t = np.arange(CHUNK)[:, None]
    j = np.arange(CHUNK)[None, :]
    lvl = np.where(j > t, -1, 0).astype(np.int32)
    for idx, m in enumerate(LEVELS):
        pair = (t // (2 * m) == j // (2 * m)) & (t // m != j // m) & (j < t)
        lvl = np.where(pair, idx + 1, lvl)
    return lvl


def _cumsum_matrix():
    t = np.arange(CHUNK)[:, None]
    i = np.arange(CHUNK)[None, :]
    return np.concatenate([i <= t, i > t], axis=0).astype(np.float32)


def _bdot(a, b):
    return jnp.dot(a, b, preferred_element_type=F32)


def _dot(a, b):
    return _bdot(a.astype(BF16), b.astype(BF16))


def _dot_nt(a, b):
    return lax.dot_general(a.astype(BF16), b.astype(BF16), (((1,), (1,)), ((), ())),
                           preferred_element_type=F32)


def _dot_tn(a, b):
    return lax.dot_general(a.astype(BF16), b.astype(BF16), (((0,), (0,)), ((), ())),
                           preferred_element_type=F32)


def _exact_left_dot(mat01, x, terms):
    n = x.shape[1]
    pieces = []
    r = x
    for _ in range(terms):
        p = r.astype(BF16)
        pieces.append(p)
        r = r - p.astype(F32)
    y = _bdot(mat01, jnp.concatenate(pieces, axis=1))
    out = y[:, 0:n]
    for i in range(1, terms):
        out = out + y[:, i * n:(i + 1) * n]
    return out


def _sigmoid(x):
    return 0.5 * jnp.tanh(0.5 * x) + 0.5


def _silu(x):
    h = 0.5 * x
    return h + h * jnp.tanh(h)


def _softplus(x):
    return jnp.maximum(x, 0.0) + jnp.log(1.0 + jnp.exp(-jnp.abs(x)))


def _rms_norm_rows(x, w):
    return x * lax.rsqrt(jnp.mean(x * x, axis=-1, keepdims=True) + EPS) * w


def _select_by_level(masks, parts):
    out = jnp.where(masks[-1], parts[-1], 0.0)
    for m, p in zip(masks[-2::-1], parts[-2::-1]):
        out = jnp.where(m, p, out)
    return out


TILE_SILU = (0, 3, 7)
TILE_LOG2_FORGET = 1
TILE_PLAIN = (2, 4, 5, 6)
N_MIX_TILES = 8


def _forget_lower_bound(lb_logits):
    ee = jnp.exp(lb_logits - jnp.max(lb_logits, axis=0, keepdims=True))
    return ee[0:1, :] / jnp.sum(ee, axis=0, keepdims=True)


def _in_proj_kernel(x_ref, nw_ref, wm_ref, wg_ref, ws_ref, lbl_ref, o_ref, os_ref, xn_ref):
    j = pl.program_id(1)

    @pl.when(j == 0)
    def _():
        xn_ref[...] = _rms_norm_rows(x_ref[...], nw_ref[...]).astype(BF16)
        os_ref[...] = jnp.dot(xn_ref[...], ws_ref[...], preferred_element_type=F32)

    def tile(w_ref, fn):
        o_ref[...] = fn(jnp.dot(xn_ref[...], w_ref[...], preferred_element_type=F32))

    def is_any(tiles):
        hit = j == tiles[0]
        for t in tiles[1:]:
            hit = hit | (j == t)
        return hit

    @pl.when(is_any(TILE_SILU))
    def _():
        tile(wm_ref, _silu)

    @pl.when(j == TILE_LOG2_FORGET)
    def _():
        lb = _forget_lower_bound(lbl_ref[...])
        tile(wm_ref, lambda a: jnp.log(lb + (1.0 - lb) * _sigmoid(a)) * LOG2_E)

    @pl.when(is_any(TILE_PLAIN))
    def _():
        tile(wm_ref, lambda a: a)

    @pl.when(j >= N_MIX_TILES)
    def _():
        tile(wg_ref, _sigmoid)


def _in_proj(x2d, norm_w, w_mix, w_gate, w_small, lb_logits, tm):
    m, d = x2d.shape
    tn = MIX_W
    n_mix_tiles = N_MIX_TILES
    assert lb_logits.shape[1] == tn and w_mix.shape[1] >= n_mix_tiles * tn
    n = n_mix_tiles * tn + w_gate.shape[1]
    ns = w_small.shape[1]
    return pl.pallas_call(
        _in_proj_kernel,
        grid=(m // tm, n // tn),
        in_specs=[
            pl.BlockSpec((tm, d), lambda i, j: (i, 0)),
            pl.BlockSpec((1, d), lambda i, j: (0, 0)),
            pl.BlockSpec((d, tn), lambda i, j: (0, jnp.minimum(j, n_mix_tiles - 1))),
            pl.BlockSpec((d, tn), lambda i, j: (0, jnp.maximum(j - n_mix_tiles, 0))),
            pl.BlockSpec((d, ns), lambda i, j: (0, 0)),
            pl.BlockSpec(lb_logits.shape, lambda i, j: (0, 0)),
        ],
        out_specs=[
            pl.BlockSpec((tm, tn), lambda i, j: (i, j)),
            pl.BlockSpec((tm, ns), lambda i, j: (i, 0)),
        ],
        out_shape=[jax.ShapeDtypeStruct((m, n), F32), jax.ShapeDtypeStruct((m, ns), F32)],
        scratch_shapes=[pltpu.VMEM((tm, d), BF16)],
        compiler_params=pltpu.CompilerParams(
            dimension_semantics=("arbitrary", "arbitrary"), vmem_limit_bytes=VMEM_LIMIT),
        name="in_proj",
    )(x2d, norm_w, w_mix, w_gate, w_small, lb_logits)


def _unit_lower_inverse_minus_eye(a_list, masks):
    e_list = [-jnp.where(masks[len(LEVELS)], a, 0.0) for a in a_list]
    for idx in range(len(LEVELS) - 2, -1, -1):
        off = [jnp.where(masks[idx + 1], a, 0.0) for a in a_list]
        x_list = [o + _dot(o, e) for o, e in zip(off, e_list)]
        e_list = [e - x - _dot(e, x) for e, x in zip(e_list, x_list)]
    return e_list


def _hgrn2_intra(p_ref, rows, row_valid, cum_mat, masks):
    tasks = []
    for r0, valid in zip(rows, row_valid):
        for h in range(HEADS):
            lo, hi = h * HEAD_DIM, (h + 1) * HEAD_DIM
            q = p_ref[0, r0:r0 + CHUNK, lo:hi]
            log_f = p_ref[0, r0:r0 + CHUNK, MIX_W + lo:MIX_W + hi]
            f = jnp.exp2(log_f)
            k = 1.0 - f
            if valid is not None:
                k = jnp.where(valid, k, 0.0)
                log_f = jnp.where(valid, log_f, 0.0)
                f = jnp.where(valid, f, 1.0)
            v = p_ref[0, r0:r0 + CHUNK, 2 * MIX_W + lo:2 * MIX_W + hi].astype(BF16)
            tasks.append(dict(q=q, f=f, k=k, log_f=log_f, v=v))
    for t in tasks:
        t["b"] = _exact_left_dot(cum_mat, t.pop("log_f"), terms=2)
    nt = lambda x, y: lax.dot_general(x, y, (((1,), (1,)), ((), ())), preferred_element_type=F32)
    for t in tasks:
        q, k, b = t["q"], t["k"], t.pop("b")
        e_cum = jnp.exp2(b)
        e_rest = jnp.exp2(jnp.broadcast_to(b[CHUNK - 1:CHUNK, :], b.shape) - b)
        qb, kb = q.astype(BF16), k.astype(BF16)
        parts = [nt(qb, kb)]
        for m in LEVELS:
            if m == 1:
                parts.append(nt(qb * t["f"].astype(BF16), kb))
            else:
                e_l = jnp.exp2(-jnp.abs(b - _block_midpoint_rows(b, m))).astype(BF16)
                parts.append(nt(qb * e_l, kb * e_l))
        t["scores"] = _select_by_level(masks, parts).astype(BF16)
        t["q_cum"] = (q * e_cum).astype(BF16)
        t["k_rest"] = (k * e_rest).astype(BF16)
        t["e_last"] = e_cum[CHUNK - 1:CHUNK, :]
        del t["q"], t["k"], t["f"]
    return tasks


def _block_midpoint_rows(b, m):
    if 2 * m >= SUBLANES:
        pieces = [jnp.broadcast_to(b[r0 + m - 1:r0 + m, :], (2 * m, b.shape[1])) for r0 in range(0, CHUNK, 2 * m)]
        return jnp.concatenate(pieces, axis=0)
    assert 4 * m == SUBLANES
    lower = lax.broadcasted_iota(jnp.int32, (SUBLANES, b.shape[1]), 0) < 2 * m
    pieces = []
    for r0 in range(0, CHUNK, SUBLANES):
        lo_half = jnp.broadcast_to(b[r0 + m - 1:r0 + m, :], (SUBLANES, b.shape[1]))
        hi_half = jnp.broadcast_to(b[r0 + 3 * m - 1:r0 + 3 * m, :], (SUBLANES, b.shape[1]))
        pieces.append(jnp.where(lower, lo_half, hi_half))
    return jnp.concatenate(pieces, axis=0)


def _gdn_intra(conv, rows, beta_all, gc_all, gc_rows, egc_all, erest_all, masks):
    strict = masks[1]
    for m in masks[2:]:
        strict = strict | m
    incl = strict | masks[0]
    tasks = []
    for si, r0 in enumerate(rows):
        for h in range(HEADS):
            lo, hi = h * HEAD_DIM, (h + 1) * HEAD_DIM
            q = _silu(conv[h][r0:r0 + CHUNK])
            k = _silu(conv[HEADS + h][r0:r0 + CHUNK])
            v = _silu(conv[2 * HEADS + h][r0:r0 + CHUNK])
            q = q * (lax.rsqrt(jnp.sum(q * q, axis=-1, keepdims=True) + EPS) * (HEAD_DIM ** -0.5))
            k = k * lax.rsqrt(jnp.sum(k * k, axis=-1, keepdims=True) + EPS)
            beta_c = beta_all[r0:r0 + CHUNK, h:h + 1]
            gc_c = gc_all[si][:, h:h + 1]
            egc_c = egc_all[si][:, h:h + 1]
            tasks.append(dict(
                q=q, k=k, beta_c=beta_c,
                rel=jnp.exp(jnp.minimum(gc_c - gc_rows[si][h:h + 1, :], 0.0)),
                rhs=jnp.concatenate([(beta_c * egc_c) * k, beta_c * v], axis=1),
                q_cum=(q * egc_c).astype(BF16),
                k_rest=(k * erest_all[si][:, h:h + 1]).astype(BF16),
                e_last=egc_all[si][CHUNK - 1:CHUNK, h:h + 1]))
    for t in tasks:
        qk_kk = _dot_nt(jnp.concatenate([t.pop("q"), t["k"]], axis=0), t.pop("k"))
        rel = t.pop("rel")
        t["attn"] = jnp.where(incl, qk_kk[0:CHUNK] * rel, 0.0).astype(BF16)
        t["a"] = jnp.where(strict, t.pop("beta_c") * qk_kk[CHUNK:2 * CHUNK] * rel, 0.0)
    e_list = _unit_lower_inverse_minus_eye([t.pop("a") for t in tasks], masks)
    for t, e in zip(tasks, e_list):
        rhs = t.pop("rhs")
        wu = rhs + _dot(e, rhs)
        t["w"] = wu[:, 0:HEAD_DIM].astype(BF16)
        t["u"] = wu[:, HEAD_DIM:2 * HEAD_DIM]
    return tasks


def _recurrence_kernel(n_pad, emit_state, cps, *refs):
    (p_ref, sm_ref, hgw_ref, cw_ref, alog_ref, dtb_ref, gdw_ref,
     cum_ref, lvl_ref, shg0_ref, sgd0_ref, tail0_ref) = refs[:12]
    if emit_state:
        oa_ref, ob_ref, shg_out, sgd_out, tail_out, shg_ref, sgd_ref, cbuf_ref = refs[12:]
    else:
        oa_ref, ob_ref, shg_ref, sgd_ref, cbuf_ref = refs[12:]
    c = pl.program_id(1)
    n_rows = cps * CHUNK
    rows = [s * CHUNK for s in range(cps)]

    @pl.when(c == 0)
    def _():
        shg_ref[...] = shg0_ref[...]
        sgd_ref[...] = sgd0_ref[...]
        for blk in range(3 * HEADS):
            cbuf_ref[blk, 0:CONV_TAIL, :] = tail0_ref[:, blk * LANES:(blk + 1) * LANES]

    lvl = lvl_ref[...]
    masks = [lvl == i for i in range(len(LEVELS) + 1)]
    cum_mat = cum_ref[...]
    if n_pad:
        row_ids = lax.broadcasted_iota(jnp.int32, (n_rows, 1), 0)
        valid_all = row_ids >= n_pad
        row_valid = [valid_all[r0:r0 + CHUNK] for r0 in rows]
    else:
        valid_all = None
        row_valid = [None] * cps

    hg = _hgrn2_intra(p_ref, rows, row_valid, cum_mat[0:CHUNK], masks)

    gd0 = 4 * MIX_W
    cw = cw_ref[...]
    first = CONV_TAIL - (CONV_K - 1)
    conv = []
    for blk in range(3 * HEADS):
        lo, hi = blk * LANES, (blk + 1) * LANES
        cbuf_ref[blk, CONV_TAIL:CONV_TAIL + n_rows, :] = p_ref[0, :, gd0 + lo:gd0 + hi]
        acc = cw[0:1, lo:hi] * cbuf_ref[blk, first:first + n_rows, :]
        for tap in range(1, CONV_K):
            acc = acc + cw[tap:tap + 1, lo:hi] * cbuf_ref[blk, first + tap:first + tap + n_rows, :]
        conv.append(acc)
        cbuf_ref[blk, 0:CONV_TAIL, :] = cbuf_ref[blk, n_rows:n_rows + CONV_TAIL, :]

    beta_all = _sigmoid(sm_ref[0, :, LANES:2 * LANES])
    g_all = -jnp.exp(alog_ref[...]) * _softplus(sm_ref[0, :, 0:LANES] + dtb_ref[...])
    if valid_all is not None:
        beta_all = jnp.where(valid_all, beta_all, 0.0)
        g_all = jnp.where(valid_all, g_all, 0.0)
    gsum = [_exact_left_dot(cum_mat, g_all[r0:r0 + CHUNK], terms=3) for r0 in rows]
    gc_all = [g[0:CHUNK] for g in gsum]
    gc_rows = [g.T for g in gc_all]
    egc_all = [jnp.exp(g) for g in gc_all]
    erest_all = [jnp.exp(g[CHUNK:2 * CHUNK]) for g in gsum]

    gd = _gdn_intra(conv, rows, beta_all, gc_all, gc_rows, egc_all, erest_all, masks)

    s_hg = [shg_ref[h] for h in range(HEADS)]
    s_gd = [sgd_ref[h] for h in range(HEADS)]
    hgw = hgw_ref[...]
    gdw = gdw_ref[...]
    for si, r0 in enumerate(rows):
        hg_t = hg[si * HEADS:(si + 1) * HEADS]
        gd_t = gd[si * HEADS:(si + 1) * HEADS]
        o_hg = [lax.dot_general(t["q_cum"], s.astype(BF16), (((1,), (1,)), ((), ())),
                                preferred_element_type=F32) + _bdot(t["scores"], t["v"])
                for t, s in zip(hg_t, s_hg)]
        s_hg = [s * t["e_last"] + _dot_tn(t["v"], t["k_rest"]) for t, s in zip(hg_t, s_hg)]
        wq_s = [_bdot(jnp.concatenate([t["w"], t["q_cum"]], axis=0), s.astype(BF16))
                for t, s in zip(gd_t, s_gd)]
        v_new = [(t["u"] - ws[0:CHUNK]).astype(BF16) for t, ws in zip(gd_t, wq_s)]
        o_gd = [ws[CHUNK:2 * CHUNK] + _bdot(t["attn"], vn) for t, ws, vn in zip(gd_t, wq_s, v_new)]
        s_gd = [s * t["e_last"] + _dot_tn(t["k_rest"], vn) for t, s, vn in zip(gd_t, s_gd, v_new)]
        for h in range(HEADS):
            lo, hi = h * HEAD_DIM, (h + 1) * HEAD_DIM
            gate = p_ref[0, r0:r0 + CHUNK, 3 * MIX_W + lo:3 * MIX_W + hi]
            oa_ref[0, r0:r0 + CHUNK, lo:hi] = (_rms_norm_rows(o_hg[h], hgw) * gate).astype(oa_ref.dtype)
            gate = p_ref[0, r0:r0 + CHUNK, gd0 + 3 * MIX_W + lo:gd0 + 3 * MIX_W + hi]
            ob_ref[0, r0:r0 + CHUNK, lo:hi] = (_rms_norm_rows(o_gd[h], gdw) * gate).astype(ob_ref.dtype)
    for h in range(HEADS):
        shg_ref[h] = s_hg[h]
        sgd_ref[h] = s_gd[h]

    if emit_state:
        @pl.when(c == pl.num_programs(1) - 1)
        def _():
            shg_out[...] = shg_ref[...]
            sgd_out[...] = sgd_ref[...]
            for blk in range(3 * HEADS):
                tail_out[:, blk * LANES:(blk + 1) * LANES] = cbuf_ref[blk, 0:CONV_TAIL, :]


def _recurrence(proj, small, hg_norm_w, conv_w, a_log, dt_bias, gd_norm_w,
                shg0, sgd0, tail0, n_pad, emit_state, cps):
    bsz, length, _ = proj.shape
    n_rows = cps * CHUNK
    cum = jnp.asarray(_cumsum_matrix(), BF16)
    lvl = jnp.asarray(_pair_level_table())
    const = lambda *shape: pl.BlockSpec(shape, lambda b, c: (0,) * len(shape))
    state_shape = (HEADS, HEAD_DIM, HEAD_DIM)
    in_specs = [
        pl.BlockSpec((1, n_rows, 8 * MIX_W), lambda b, c: (b, c, 0)),
        pl.BlockSpec((1, n_rows, 2 * LANES), lambda b, c: (b, c, 0)),
        const(1, HEAD_DIM), const(CONV_K, 3 * MIX_W),
        const(1, LANES), const(1, LANES), const(1, HEAD_DIM),
        const(*cum.shape), const(CHUNK, CHUNK),
        const(*state_shape), const(*state_shape), const(CONV_TAIL, 3 * MIX_W),
    ]
    out_specs = [
        pl.BlockSpec((1, n_rows, MIX_W), lambda b, c: (b, c, 0)),
        pl.BlockSpec((1, n_rows, MIX_W), lambda b, c: (b, c, 0)),
    ]
    out_shape = [jax.ShapeDtypeStruct((bsz, length, MIX_W), BF16),
                 jax.ShapeDtypeStruct((bsz, length, MIX_W), BF16)]
    if emit_state:
        out_specs += [const(*state_shape), const(*state_shape), const(CONV_TAIL, 3 * MIX_W)]
        out_shape += [jax.ShapeDtypeStruct(state_shape, F32), jax.ShapeDtypeStruct(state_shape, F32),
                      jax.ShapeDtypeStruct((CONV_TAIL, 3 * MIX_W), F32)]
    return pl.pallas_call(
        functools.partial(_recurrence_kernel, n_pad, emit_state, cps),
        grid=(bsz, length // n_rows),
        in_specs=in_specs,
        out_specs=out_specs,
        out_shape=out_shape,
        scratch_shapes=[pltpu.VMEM(state_shape, F32), pltpu.VMEM(state_shape, F32),
                        pltpu.VMEM((3 * HEADS, CONV_TAIL + n_rows, LANES), F32)],
        compiler_params=pltpu.CompilerParams(
            dimension_semantics=("arbitrary", "arbitrary"), vmem_limit_bytes=VMEM_LIMIT),
        name="recurrence_meta" if emit_state else "recurrence",
    )(proj, small, hg_norm_w, conv_w, a_log, dt_bias, gd_norm_w,
      cum, lvl, shg0, sgd0, tail0)


def _merge_kernel(oa_ref, ob_ref, g_ref, x_ref, wa_ref, wb_ref, wo_ref, h_ref):
    d = x_ref.shape[1]
    ya = jnp.dot(oa_ref[...], wa_ref[...], preferred_element_type=F32)
    yb = jnp.dot(ob_ref[...], wb_ref[...], preferred_element_type=F32)
    merged = g_ref[:, 0:d] * ya + g_ref[:, d:2 * d] * yb
    h_ref[...] = x_ref[...] + jnp.dot(merged.astype(BF16), wo_ref[...], preferred_element_type=F32)


def _merge(oa, ob, proj, gate_block, x2d, wa, wb, wo, tm):
    m, d = x2d.shape
    single = dict(pipeline_mode=pl.Buffered(1))
    return pl.pallas_call(
        _merge_kernel,
        grid=(m // tm,),
        in_specs=[
            pl.BlockSpec((tm, MIX_W), lambda i: (i, 0)),
            pl.BlockSpec((tm, MIX_W), lambda i: (i, 0)),
            pl.BlockSpec((tm, 2 * d), lambda i: (i, gate_block)),
            pl.BlockSpec((tm, d), lambda i: (i, 0)),
            pl.BlockSpec((MIX_W, d), lambda i: (0, 0), **single),
            pl.BlockSpec((MIX_W, d), lambda i: (0, 0), **single),
            pl.BlockSpec((d, d), lambda i: (0, 0), **single),
        ],
        out_specs=pl.BlockSpec((tm, d), lambda i: (i, 0)),
        out_shape=jax.ShapeDtypeStruct((m, d), F32),
        compiler_params=pltpu.CompilerParams(
            dimension_semantics=("arbitrary",), vmem_limit_bytes=VMEM_LIMIT),
        name="merge",
    )(oa, ob, proj, x2d, wa, wb, wo)


def _ffn_kernel(h_ref, wg_ref, wu_ref, wd_ref, nw_ref, fw_ref, o_ref, xn_ref, hid_ref):
    j = pl.program_id(1)
    nf = pl.num_programs(1) - 1

    def gate_up(slot):
        xn = xn_ref[...]
        gate = jnp.dot(xn, wg_ref[...], preferred_element_type=F32)
        up = jnp.dot(xn, wu_ref[...], preferred_element_type=F32)
        hid_ref[slot] = (_silu(gate) * up).astype(BF16)

    def down(slot):
        o_ref[...] += jnp.dot(hid_ref[slot], wd_ref[...], preferred_element_type=F32)

    @pl.when(j == 0)
    def _():
        h = h_ref[...]
        o_ref[...] = h
        xn_ref[...] = _rms_norm_rows(h, nw_ref[...]).astype(BF16)
        gate_up(0)

    @pl.when((j > 0) & (j < nf))
    def _():
        gate_up(j % 2)
        down((j - 1) % 2)

    @pl.when(j == nf)
    def _():
        down((j - 1) % 2)
        o_ref[...] = _rms_norm_rows(o_ref[...], fw_ref[...])


def _ffn(h, w_gate_up, w_down, ffn_norm_w, final_norm_w, tm, tf):
    m, d = h.shape
    ff = w_down.shape[0]
    nf = ff // tf
    return pl.pallas_call(
        _ffn_kernel,
        grid=(m // tm, nf + 1),
        in_specs=[
            pl.BlockSpec((tm, d), lambda i, j: (i, 0)),
            pl.BlockSpec((d, tf), lambda i, j: (0, jnp.minimum(j, nf - 1))),
            pl.BlockSpec((d, tf), lambda i, j: (0, nf + jnp.minimum(j, nf - 1))),
            pl.BlockSpec((tf, d), lambda i, j: (jnp.maximum(j - 1, 0), 0)),
            pl.BlockSpec((1, d), lambda i, j: (0, 0)),
            pl.BlockSpec((1, d), lambda i, j: (0, 0)),
        ],
        out_specs=pl.BlockSpec((tm, d), lambda i, j: (i, 0)),
        out_shape=jax.ShapeDtypeStruct((m, d), F32),
        scratch_shapes=[pltpu.VMEM((tm, d), BF16), pltpu.VMEM((2, tm, tf), BF16)],
        compiler_params=pltpu.CompilerParams(
            dimension_semantics=("arbitrary", "arbitrary"), vmem_limit_bytes=VMEM_LIMIT_FFN),
        name="ffn",
    )(h, w_gate_up, w_gate_up, w_down, ffn_norm_w, final_norm_w)


def _pad_lanes(v):
    return jnp.pad(v.astype(F32), (0, LANES - v.shape[0]))[None, :]


def kernel(x, meta_tokens, lb_logits, mix_norm_w, w_in, hg_norm_w, gd_conv_w, gd_a_log, gd_dt_bias,
           gd_norm_w, w_branch_a, w_branch_b, w_out, ffn_norm_w, w_ffn_in, w_ffn_out, final_norm_w):
    bsz, seq, d = x.shape
    layer = 0
    assert mix_norm_w.shape[0] == 1, "single-layer block"
    assert seq % (REC_CHUNKS_PER_STEP * CHUNK) == 0 and meta_tokens.shape[0] == N_META <= CHUNK

    w = w_in[layer]
    n_mix = 8 * MIX_W
    assert n_mix == N_MIX_TILES * MIX_W
    w_bf = w.astype(BF16)
    w_gate = w_bf[:, n_mix + 2 * HEADS:]
    pad = jnp.zeros((d, LANES - HEADS), BF16)
    w_small = jnp.concatenate([w_bf[:, n_mix:n_mix + HEADS], pad,
                               w_bf[:, n_mix + HEADS:n_mix + 2 * HEADS], pad], axis=1)
    norm_w = mix_norm_w[layer][None, :]
    lb_logits = lb_logits.astype(F32)
    rec_params = (hg_norm_w[layer][None, :], gd_conv_w[layer],
                  _pad_lanes(gd_a_log[layer]), _pad_lanes(gd_dt_bias[layer]), gd_norm_w[layer][None, :])

    n_pad = CHUNK - N_META
    x_meta = jnp.concatenate([jnp.zeros((n_pad, d), x.dtype), meta_tokens.astype(x.dtype)], axis=0)
    proj_m, small_m = _in_proj(x_meta, norm_w, w_bf, w_gate, w_small, lb_logits, tm=CHUNK)
    zero_state = jnp.zeros((HEADS, HEAD_DIM, HEAD_DIM), F32)
    zero_tail = jnp.zeros((CONV_TAIL, 3 * MIX_W), F32)
    _, _, shg0, sgd0, tail0 = _recurrence(proj_m[None], small_m[None], *rec_params,
                                          zero_state, zero_state, zero_tail,
                                          n_pad=n_pad, emit_state=True, cps=1)

    x2d = x.reshape(bsz * seq, d)
    proj, small = _in_proj(x2d, norm_w, w_bf, w_gate, w_small, lb_logits, tm=min(1024, bsz * seq))
    oa, ob = _recurrence(proj.reshape(bsz, seq, -1), small.reshape(bsz, seq, -1), *rec_params,
                         shg0, sgd0, tail0, n_pad=0, emit_state=False, cps=REC_CHUNKS_PER_STEP)

    h1 = _merge(oa.reshape(bsz * seq, MIX_W), ob.reshape(bsz * seq, MIX_W), proj,
                n_mix // (2 * d), x2d, w_branch_a[layer].astype(BF16), w_branch_b[layer].astype(BF16),
                w_out[layer].astype(BF16), tm=min(256, bsz * seq))
    out = _ffn(h1, w_ffn_in[layer].astype(BF16), w_ffn_out[layer].astype(BF16),
               ffn_norm_w[layer][None, :], final_norm_w[None, :], tm=min(1024, bsz * seq), tf=512)
    return out.reshape(bsz, seq, d)
```

```python
import functools

import numpy as np
import jax
import jax.numpy as jnp
from jax import lax
from jax.experimental import pallas as pl
from jax.experimental.pallas import tpu as pltpu

F32 = jnp.float32
BF16 = jnp.bfloat16

EPS = 1e-6
LOG2_E = 1.4426950408889634
N_META = 16
CHUNK = 64
HEADS = 8
HEAD_DIM = 128
MIX_W = HEADS * HEAD_DIM
REC_CHUNKS_PER_STEP = 4
GDN_GROUP_CHUNKS = 2
META_CHUNKS = 2
CONV_K = 4
CONV_TAIL = 8
LANES = 128
SUBLANES = 8
VMEM_LIMIT = 56 * 1024 * 1024
VMEM_LIMIT_FFN = 60 * 1024 * 1024

LEVELS = tuple(CHUNK >> (i + 1) for i in range(CHUNK.bit_length() - 1))


def _pair_level_table():
    t = np.arange(CHUNK)[:, None]
    j = np.arange(CHUNK)[None, :]
    lvl = np.where(j > t, -1, 0).astype(np.int32)
    for idx, m in enumerate(LEVELS):
        pair = (t // (2 * m) == j // (2 * m)) & (t // m != j // m) & (j < t)
        lvl = np.where(pair, idx + 1, lvl)
    return lvl


def _cumsum_matrix():
    t = np.arange(CHUNK)[:, None]
    i = np.arange(CHUNK)[None, :]
    return np.concatenate([i <= t, i > t], axis=0).astype(np.float32)


def _bdot(a, b):
    return jnp.dot(a, b, preferred_element_type=F32)


def _dot(a, b):
    return _bdot(a.astype(BF16), b.astype(BF16))


def _dot_nt(a, b):
    return lax.dot_general(a.astype(BF16), b.astype(BF16), (((1,), (1,)), ((), ())),
                           preferred_element_type=F32)


def _dot_tn(a, b):
    return lax.dot_general(a.astype(BF16), b.astype(BF16), (((0,), (0,)), ((), ())),
                           preferred_element_type=F32)


def _exact_left_dot(mat01, x, terms):
    n = x.shape[1]
    pieces = []
    r = x
    for _ in range(terms):
        p = r.astype(BF16)
        pieces.append(p)
        r = r - p.astype(F32)
    y = _bdot(mat01, jnp.concatenate(pieces, axis=1))
    out = y[:, 0:n]
    for i in range(1, terms):
        out = out + y[:, i * n:(i + 1) * n]
    return out


def _sigmoid(x):
    return 0.5 * jnp.tanh(0.5 * x) + 0.5


def _silu(x):
    h = 0.5 * x
    return h + h * jnp.tanh(h)


def _softplus(x):
    return jnp.maximum(x, 0.0) + jnp.log(1.0 + jnp.exp(-jnp.abs(x)))


def _rms_norm_rows(x, w):
    return x * lax.rsqrt(jnp.mean(x * x, axis=-1, keepdims=True) + EPS) * w


def _select_by_level(masks, parts):
    out = jnp.where(masks[-1], parts[-1], 0.0)
    for m, p in zip(masks[-2::-1], parts[-2::-1]):
        out = jnp.where(m, p, out)
    return out


TILE_SILU = (0, 3, 7)
TILE_LOG2_FORGET = 1
TILE_PLAIN = (2, 4, 5, 6)
N_MIX_TILES = 8


def _forget_lower_bound(lb_logits):
    ee = jnp.exp(lb_logits - jnp.max(lb_logits, axis=0, keepdims=True))
    return ee[0:1, :] / jnp.sum(ee, axis=0, keepdims=True)


def _in_proj_kernel(x_ref, nw_ref, wm_ref, wg_ref, ws_ref, lbl_ref, o_ref, os_ref, xn_ref):
    j = pl.program_id(1)

    @pl.when(j == 0)
    def _():
        xn_ref[...] = _rms_norm_rows(x_ref[...], nw_ref[...]).astype(BF16)
        os_ref[...] = jnp.dot(xn_ref[...], ws_ref[...], preferred_element_type=F32)

    def tile(w_ref, fn):
        o_ref[...] = fn(jnp.dot(xn_ref[...], w_ref[...], preferred_element_type=F32))

    def is_any(tiles):
        hit = j == tiles[0]
        for t in tiles[1:]:
            hit = hit | (j == t)
        return hit

    @pl.when(is_any(TILE_SILU))
    def _():
        tile(wm_ref, _silu)

    @pl.when(j == TILE_LOG2_FORGET)
    def _():
        lb = _forget_lower_bound(lbl_ref[...])
        tile(wm_ref, lambda a: jnp.log(lb + (1.0 - lb) * _sigmoid(a)) * LOG2_E)

    @pl.when(is_any(TILE_PLAIN))
    def _():
        tile(wm_ref, lambda a: a)

    @pl.when(j >= N_MIX_TILES)
    def _():
        tile(wg_ref, _sigmoid)


def _in_proj(x2d, norm_w, w_mix, w_gate, w_small, lb_logits, tm):
    m, d = x2d.shape
    tn = MIX_W
    n_mix_tiles = N_MIX_TILES
    assert lb_logits.shape[1] == tn and w_mix.shape[1] >= n_mix_tiles * tn
    n = n_mix_tiles * tn + w_gate.shape[1]
    ns = w_small.shape[1]
    return pl.pallas_call(
        _in_proj_kernel,
        grid=(m // tm, n // tn),
        in_specs=[
            pl.BlockSpec((tm, d), lambda i, j: (i, 0)),
            pl.BlockSpec((1, d), lambda i, j: (0, 0)),
            pl.BlockSpec((d, tn), lambda i, j: (0, jnp.minimum(j, n_mix_tiles - 1))),
            pl.BlockSpec((d, tn), lambda i, j: (0, jnp.maximum(j - n_mix_tiles, 0))),
            pl.BlockSpec((d, ns), lambda i, j: (0, 0)),
            pl.BlockSpec(lb_logits.shape, lambda i, j: (0, 0)),
        ],
        out_specs=[
            pl.BlockSpec((tm, tn), lambda i, j: (i, j)),
            pl.BlockSpec((tm, ns), lambda i, j: (i, 0)),
        ],
        out_shape=[jax.ShapeDtypeStruct((m, n), F32), jax.ShapeDtypeStruct((m, ns), F32)],
        scratch_shapes=[pltpu.VMEM((tm, d), BF16)],
        compiler_params=pltpu.CompilerParams(
            dimension_semantics=("arbitrary", "arbitrary"), vmem_limit_bytes=VMEM_LIMIT),
        name="in_proj",
    )(x2d, norm_w, w_mix, w_gate, w_small, lb_logits)


def _unit_lower_inverse_minus_eye(a_list, masks):
    e_list = [-jnp.where(masks[len(LEVELS)], a, 0.0) for a in a_list]
    for idx in range(len(LEVELS) - 2, -1, -1):
        off = [jnp.where(masks[idx + 1], a, 0.0) for a in a_list]
        x_list = [o + _dot(o, e) for o, e in zip(off, e_list)]
        e_list = [e - x - _dot(e, x) for e, x in zip(e_list, x_list)]
    return e_list


def _hgrn2_intra(p_ref, rows, row_valid, cum_mat, masks):
    tasks = []
    for r0, valid in zip(rows, row_valid):
        for h in range(HEADS):
            lo, hi = h * HEAD_DIM, (h + 1) * HEAD_DIM
            q = p_ref[0, r0:r0 + CHUNK, lo:hi]
            log_f = p_ref[0, r0:r0 + CHUNK, MIX_W + lo:MIX_W + hi]
            f = jnp.exp2(log_f)
            k = 1.0 - f
            if valid is not None:
                k = jnp.where(valid, k, 0.0)
                log_f = jnp.where(valid, log_f, 0.0)
                f = jnp.where(valid, f, 1.0)
            v = p_ref[0, r0:r0 + CHUNK, 2 * MIX_W + lo:2 * MIX_W + hi].astype(BF16)
            tasks.append(dict(q=q, f=f, k=k, log_f=log_f, v=v))
    for t in tasks:
        t["b"] = _exact_left_dot(cum_mat, t.pop("log_f"), terms=2)
    nt = lambda x, y: lax.dot_general(x, y, (((1,), (1,)), ((), ())), preferred_element_type=F32)
    for t in tasks:
        q, k, b = t["q"], t["k"], t.pop("b")
        e_cum = jnp.exp2(b)
        e_rest = jnp.exp2(jnp.broadcast_to(b[CHUNK - 1:CHUNK, :], b.shape) - b)
        qb, kb = q.astype(BF16), k.astype(BF16)
        parts = [nt(qb, kb)]
        for m in LEVELS:
            if m == 1:
                parts.append(nt(qb * t["f"].astype(BF16), kb))
            else:
                e_l = jnp.exp2(-jnp.abs(b - _block_midpoint_rows(b, m))).astype(BF16)
                parts.append(nt(qb * e_l, kb * e_l))
        t["scores"] = _select_by_level(masks, parts).astype(BF16)
        t["q_cum"] = (q * e_cum).astype(BF16)
        t["k_rest"] = (k * e_rest).astype(BF16)
        t["e_last"] = e_cum[CHUNK - 1:CHUNK, :]
        del t["q"], t["k"], t["f"]
    return tasks


def _block_midpoint_rows(b, m):
    if 2 * m >= SUBLANES:
        pieces = [jnp.broadcast_to(b[r0 + m - 1:r0 + m, :], (2 * m, b.shape[1])) for r0 in range(0, CHUNK, 2 * m)]
        return jnp.concatenate(pieces, axis=0)
    assert 4 * m == SUBLANES
    lower = lax.broadcasted_iota(jnp.int32, (SUBLANES, b.shape[1]), 0) < 2 * m
    pieces = []
    for r0 in range(0, CHUNK, SUBLANES):
        lo_half = jnp.broadcast_to(b[r0 + m - 1:r0 + m, :], (SUBLANES, b.shape[1]))
        hi_half = jnp.broadcast_to(b[r0 + 3 * m - 1:r0 + 3 * m, :], (SUBLANES, b.shape[1]))
        pieces.append(jnp.where(lower, lo_half, hi_half))
    return jnp.concatenate(pieces, axis=0)


def _gdn_intra(conv, rows, beta_all, gc_all, gc_rows, egc_all, erest_all, masks):
    strict = masks[1]
    for m in masks[2:]:
        strict = strict | m
    incl = strict | masks[0]
    tasks = []
    for si, r0 in enumerate(rows):
        for h in range(HEADS):
            lo, hi = h * HEAD_DIM, (h + 1) * HEAD_DIM
            q = _silu(conv[h][r0:r0 + CHUNK])
            k = _silu(conv[HEADS + h][r0:r0 + CHUNK])
            v = _silu(conv[2 * HEADS + h][r0:r0 + CHUNK])
            q = q * (lax.rsqrt(jnp.sum(q * q, axis=-1, keepdims=True) + EPS) * (HEAD_DIM ** -0.5))
            k = k * lax.rsqrt(jnp.sum(k * k, axis=-1, keepdims=True) + EPS)
            beta_c = beta_all[r0:r0 + CHUNK, h:h + 1]
            gc_c = gc_all[si][:, h:h + 1]
            egc_c = egc_all[si][:, h:h + 1]
            tasks.append(dict(
                q=q, k=k, beta_c=beta_c,
                rel=jnp.exp(jnp.minimum(gc_c - gc_rows[si][h:h + 1, :], 0.0)),
                rhs=jnp.concatenate([(beta_c * egc_c) * k, beta_c * v], axis=1),
                q_cum=(q * egc_c).astype(BF16),
                k_rest=(k * erest_all[si][:, h:h + 1]).astype(BF16),
                e_last=egc_all[si][CHUNK - 1:CHUNK, h:h + 1]))
    for t in tasks:
        qk_kk = _dot_nt(jnp.concatenate([t.pop("q"), t["k"]], axis=0), t.pop("k"))
        rel = t.pop("rel")
        t["attn"] = jnp.where(incl, qk_kk[0:CHUNK] * rel, 0.0).astype(BF16)
        t["a"] = jnp.where(strict, t.pop("beta_c") * qk_kk[CHUNK:2 * CHUNK] * rel, 0.0)
    e_list = _unit_lower_inverse_minus_eye([t.pop("a") for t in tasks], masks)
    for t, e in zip(tasks, e_list):
        rhs = t.pop("rhs")
        wu = rhs + _dot(e, rhs)
        t["w"] = wu[:, 0:HEAD_DIM].astype(BF16)
        t["u"] = wu[:, HEAD_DIM:2 * HEAD_DIM]
    return tasks


def _recurrence_kernel(n_pad, emit_state, cps, *refs):
    (p_ref, sm_ref, hgw_ref, cw_ref, alog_ref, dtb_ref, gdw_ref,
     cum_ref, lvl_ref, shg0_ref, sgd0_ref, tail0_ref) = refs[:12]
    if emit_state:
        oa_ref, ob_ref, shg_out, sgd_out, tail_out, shg_ref, sgd_ref, cbuf_ref = refs[12:]
    else:
        oa_ref, ob_ref, shg_ref, sgd_ref, cbuf_ref = refs[12:]
    c = pl.program_id(1)
    n_rows = cps * CHUNK
    rows = [s * CHUNK for s in range(cps)]

    @pl.when(c == 0)
    def _():
        shg_ref[...] = shg0_ref[...]
        sgd_ref[...] = sgd0_ref[...]
        for blk in range(3 * HEADS):
            cbuf_ref[blk, 0:CONV_TAIL, :] = tail0_ref[:, blk * LANES:(blk + 1) * LANES]

    lvl = lvl_ref[...]
    masks = [lvl == i for i in range(len(LEVELS) + 1)]
    cum_mat = cum_ref[...]
    if n_pad:
        row_ids = lax.broadcasted_iota(jnp.int32, (n_rows, 1), 0)
        valid_all = row_ids >= n_pad
        row_valid = [valid_all[r0:r0 + CHUNK] for r0 in rows]
    else:
        valid_all = None
        row_valid = [None] * cps

    hg = _hgrn2_intra(p_ref, rows, row_valid, cum_mat[0:CHUNK], masks)

    gd0 = 4 * MIX_W
    cw = cw_ref[...]
    first = CONV_TAIL - (CONV_K - 1)
    conv = []
    for blk in range(3 * HEADS):
        lo, hi = blk * LANES, (blk + 1) * LANES
        cbuf_ref[blk, CONV_TAIL:CONV_TAIL + n_rows, :] = p_ref[0, :, gd0 + lo:gd0 + hi]
        acc = cw[0:1, lo:hi] * cbuf_ref[blk, first:first + n_rows, :]
        for tap in range(1, CONV_K):
            acc = acc + cw[tap:tap + 1, lo:hi] * cbuf_ref[blk, first + tap:first + tap + n_rows, :]
        conv.append(acc)
        cbuf_ref[blk, 0:CONV_TAIL, :] = cbuf_ref[blk, n_rows:n_rows + CONV_TAIL, :]

    beta_all = _sigmoid(sm_ref[0, :, LANES:2 * LANES])
    g_all = -jnp.exp(alog_ref[...]) * _softplus(sm_ref[0, :, 0:LANES] + dtb_ref[...])
    if valid_all is not None:
        beta_all = jnp.where(valid_all, beta_all, 0.0)
        g_all = jnp.where(valid_all, g_all, 0.0)
    gsum = [_exact_left_dot(cum_mat, g_all[r0:r0 + CHUNK], terms=3) for r0 in rows]
    gc_all = [g[0:CHUNK] for g in gsum]
    gc_rows = [g.T for g in gc_all]
    egc_all = [jnp.exp(g) for g in gc_all]
    erest_all = [jnp.exp(g[CHUNK:2 * CHUNK]) for g in gsum]

    gd = []
    for g0 in range(0, cps, GDN_GROUP_CHUNKS):
        sl = slice(g0, g0 + GDN_GROUP_CHUNKS)
        gd += _gdn_intra(conv, rows[sl], beta_all, gc_all[sl], gc_rows[sl], egc_all[sl], erest_all[sl], masks)

    s_hg = [shg_ref[h] for h in range(HEADS)]
    s_gd = [sgd_ref[h] for h in range(HEADS)]
    hgw = hgw_ref[...]
    gdw = gdw_ref[...]
    for si, r0 in enumerate(rows):
        hg_t = hg[si * HEADS:(si + 1) * HEADS]
        gd_t = gd[si * HEADS:(si + 1) * HEADS]
        o_hg = [lax.dot_general(t["q_cum"], s.astype(BF16), (((1,), (1,)), ((), ())),
                                preferred_element_type=F32) + _bdot(t["scores"], t["v"])
                for t, s in zip(hg_t, s_hg)]
        s_hg = [s * t["e_last"] + _dot_tn(t["v"], t["k_rest"]) for t, s in zip(hg_t, s_hg)]
        wq_s = [_bdot(jnp.concatenate([t["w"], t["q_cum"]], axis=0), s.astype(BF16))
                for t, s in zip(gd_t, s_gd)]
        v_new = [(t["u"] - ws[0:CHUNK]).astype(BF16) for t, ws in zip(gd_t, wq_s)]
        o_gd = [ws[CHUNK:2 * CHUNK] + _bdot(t["attn"], vn) for t, ws, vn in zip(gd_t, wq_s, v_new)]
        s_gd = [s * t["e_last"] + _dot_tn(t["k_rest"], vn) for t, s, vn in zip(gd_t, s_gd, v_new)]
        for h in range(HEADS):
            lo, hi = h * HEAD_DIM, (h + 1) * HEAD_DIM
            gate = p_ref[0, r0:r0 + CHUNK, 3 * MIX_W + lo:3 * MIX_W + hi]
            oa_ref[0, r0:r0 + CHUNK, lo:hi] = (_rms_norm_rows(o_hg[h], hgw) * gate).astype(oa_ref.dtype)
            gate = p_ref[0, r0:r0 + CHUNK, gd0 + 3 * MIX_W + lo:gd0 + 3 * MIX_W + hi]
            ob_ref[0, r0:r0 + CHUNK, lo:hi] = (_rms_norm_rows(o_gd[h], gdw) * gate).astype(ob_ref.dtype)
    for h in range(HEADS):
        shg_ref[h] = s_hg[h]
        sgd_ref[h] = s_gd[h]

    if emit_state:
        @pl.when(c == pl.num_programs(1) - 1)
        def _():
            shg_out[...] = shg_ref[...]
            sgd_out[...] = sgd_ref[...]
            for blk in range(3 * HEADS):
                tail_out[:, blk * LANES:(blk + 1) * LANES] = cbuf_ref[blk, 0:CONV_TAIL, :]


def _recurrence(proj, small, hg_norm_w, conv_w, a_log, dt_bias, gd_norm_w,
                shg0, sgd0, tail0, n_pad, emit_state, cps):
    bsz, length, _ = proj.shape
    n_rows = cps * CHUNK
    cum = jnp.asarray(_cumsum_matrix(), BF16)
    lvl = jnp.asarray(_pair_level_table())
    const = lambda *shape: pl.BlockSpec(shape, lambda b, c: (0,) * len(shape))
    state_shape = (HEADS, HEAD_DIM, HEAD_DIM)
    in_specs = [
        pl.BlockSpec((1, n_rows, 8 * MIX_W), lambda b, c: (b, c, 0)),
        pl.BlockSpec((1, n_rows, 2 * LANES), lambda b, c: (b, c, 0)),
        const(1, HEAD_DIM), const(CONV_K, 3 * MIX_W),
        const(1, LANES), const(1, LANES), const(1, HEAD_DIM),
        const(*cum.shape), const(CHUNK, CHUNK),
        const(*state_shape), const(*state_shape), const(CONV_TAIL, 3 * MIX_W),
    ]
    out_specs = [
        pl.BlockSpec((1, n_rows, MIX_W), lambda b, c: (b, c, 0)),
        pl.BlockSpec((1, n_rows, MIX_W), lambda b, c: (b, c, 0)),
    ]
    out_shape = [jax.ShapeDtypeStruct((bsz, length, MIX_W), BF16),
                 jax.ShapeDtypeStruct((bsz, length, MIX_W), BF16)]
    if emit_state:
        out_specs += [const(*state_shape), const(*state_shape), const(CONV_TAIL, 3 * MIX_W)]
        out_shape += [jax.ShapeDtypeStruct(state_shape, F32), jax.ShapeDtypeStruct(state_shape, F32),
                      jax.ShapeDtypeStruct((CONV_TAIL, 3 * MIX_W), F32)]
    return pl.pallas_call(
        functools.partial(_recurrence_kernel, n_pad, emit_state, cps),
        grid=(bsz, length // n_rows),
        in_specs=in_specs,
        out_specs=out_specs,
        out_shape=out_shape,
        scratch_shapes=[pltpu.VMEM(state_shape, F32), pltpu.VMEM(state_shape, F32),
                        pltpu.VMEM((3 * HEADS, CONV_TAIL + n_rows, LANES), F32)],
        compiler_params=pltpu.CompilerParams(
            dimension_semantics=("arbitrary", "arbitrary"), vmem_limit_bytes=VMEM_LIMIT),
        name="recurrence_meta" if emit_state else "recurrence",
    )(proj, small, hg_norm_w, conv_w, a_log, dt_bias, gd_norm_w,
      cum, lvl, shg0, sgd0, tail0)


def _merge_kernel(oa_ref, ob_ref, g_ref, x_ref, wa_ref, wb_ref, wo_ref, h_ref):
    d = x_ref.shape[1]
    ya = jnp.dot(oa_ref[...], wa_ref[...], preferred_element_type=F32)
    yb = jnp.dot(ob_ref[...], wb_ref[...], preferred_element_type=F32)
    merged = g_ref[:, 0:d] * ya + g_ref[:, d:2 * d] * yb
    h_ref[...] = x_ref[...] + jnp.dot(merged.astype(BF16), wo_ref[...], preferred_element_type=F32)


def _merge(oa, ob, proj, gate_block, x2d, wa, wb, wo, tm):
    m, d = x2d.shape
    single = dict(pipeline_mode=pl.Buffered(1))
    return pl.pallas_call(
        _merge_kernel,
        grid=(m // tm,),
        in_specs=[
            pl.BlockSpec((tm, MIX_W), lambda i: (i, 0)),
            pl.BlockSpec((tm, MIX_W), lambda i: (i, 0)),
            pl.BlockSpec((tm, 2 * d), lambda i: (i, gate_block)),
            pl.BlockSpec((tm, d), lambda i: (i, 0)),
            pl.BlockSpec((MIX_W, d), lambda i: (0, 0), **single),
            pl.BlockSpec((MIX_W, d), lambda i: (0, 0), **single),
            pl.BlockSpec((d, d), lambda i: (0, 0), **single),
        ],
        out_specs=pl.BlockSpec((tm, d), lambda i: (i, 0)),
        out_shape=jax.ShapeDtypeStruct((m, d), F32),
        compiler_params=pltpu.CompilerParams(
            dimension_semantics=("arbitrary",), vmem_limit_bytes=VMEM_LIMIT),
        name="merge",
    )(oa, ob, proj, x2d, wa, wb, wo)


def _ffn_kernel(h_ref, wg_ref, wu_ref, wd_ref, nw_ref, fw_ref, o_ref, xn_ref, hid_ref):
    j = pl.program_id(1)
    nf = pl.num_programs(1) - 1

    def gate_up(slot):
        xn = xn_ref[...]
        gate = jnp.dot(xn, wg_ref[...], preferred_element_type=F32)
        up = jnp.dot(xn, wu_ref[...], preferred_element_type=F32)
        hid_ref[slot] = (_silu(gate) * up).astype(BF16)

    def down(slot):
        o_ref[...] += jnp.dot(hid_ref[slot], wd_ref[...], preferred_element_type=F32)

    @pl.when(j == 0)
    def _():
        h = h_ref[...]
        o_ref[...] = h
        xn_ref[...] = _rms_norm_rows(h, nw_ref[...]).astype(BF16)
        gate_up(0)

    @pl.when((j > 0) & (j < nf))
    def _():
        gate_up(j % 2)
        down((j - 1) % 2)

    @pl.when(j == nf)
    def _():
        down((j - 1) % 2)
        o_ref[...] = _rms_norm_rows(o_ref[...], fw_ref[...])


def _ffn(h, w_gate_up, w_down, ffn_norm_w, final_norm_w, tm, tf):
    m, d = h.shape
    ff = w_down.shape[0]
    nf = ff // tf
    return pl.pallas_call(
        _ffn_kernel,
        grid=(m // tm, nf + 1),
        in_specs=[
            pl.BlockSpec((tm, d), lambda i, j: (i, 0)),
            pl.BlockSpec((d, tf), lambda i, j: (0, jnp.minimum(j, nf - 1))),
            pl.BlockSpec((d, tf), lambda i, j: (0, nf + jnp.minimum(j, nf - 1))),
            pl.BlockSpec((tf, d), lambda i, j: (jnp.maximum(j - 1, 0), 0)),
            pl.BlockSpec((1, d), lambda i, j: (0, 0)),
            pl.BlockSpec((1, d), lambda i, j: (0, 0)),
        ],
        out_specs=pl.BlockSpec((tm, d), lambda i, j: (i, 0)),
        out_shape=jax.ShapeDtypeStruct((m, d), F32),
        scratch_shapes=[pltpu.VMEM((tm, d), BF16), pltpu.VMEM((2, tm, tf), BF16)],
        compiler_params=pltpu.CompilerParams(
            dimension_semantics=("arbitrary", "arbitrary"), vmem_limit_bytes=VMEM_LIMIT_FFN),
        name="ffn",
    )(h, w_gate_up, w_gate_up, w_down, ffn_norm_w, final_norm_w)


def _pad_lanes(v):
    return jnp.pad(v.astype(F32), (0, LANES - v.shape[0]))[None, :]


def kernel(x, meta_tokens, lb_logits, mix_norm_w, w_in, hg_norm_w, gd_conv_w, gd_a_log, gd_dt_bias,
           gd_norm_w, w_branch_a, w_branch_b, w_out, ffn_norm_w, w_ffn_in, w_ffn_out, final_norm_w):
    bsz, seq, d = x.shape
    layer = 0
    assert mix_norm_w.shape[0] == 1, "single-layer block"
    assert seq % (REC_CHUNKS_PER_STEP * CHUNK) == 0 and meta_tokens.shape[0] == N_META <= CHUNK

    w = w_in[layer]
    n_mix = 8 * MIX_W
    assert n_mix == N_MIX_TILES * MIX_W
    w_bf = w.astype(BF16)
    w_gate = w_bf[:, n_mix + 2 * HEADS:]
    pad = jnp.zeros((d, LANES - HEADS), BF16)
    w_small = jnp.concatenate([w_bf[:, n_mix:n_mix + HEADS], pad,
                               w_bf[:, n_mix + HEADS:n_mix + 2 * HEADS], pad], axis=1)
    norm_w = mix_norm_w[layer][None, :]
    lb_logits = lb_logits.astype(F32)
    rec_params = (hg_norm_w[layer][None, :], gd_conv_w[layer],
                  _pad_lanes(gd_a_log[layer]), _pad_lanes(gd_dt_bias[layer]), gd_norm_w[layer][None, :])

    n_pad = META_CHUNKS * CHUNK - N_META
    x_meta = jnp.concatenate([jnp.zeros((n_pad, d), x.dtype), meta_tokens.astype(x.dtype)], axis=0)
    proj_m, small_m = _in_proj(x_meta, norm_w, w_bf, w_gate, w_small, lb_logits, tm=META_CHUNKS * CHUNK)
    zero_state = jnp.zeros((HEADS, HEAD_DIM, HEAD_DIM), F32)
    zero_tail = jnp.zeros((CONV_TAIL, 3 * MIX_W), F32)
    _, _, shg0, sgd0, tail0 = _recurrence(proj_m[None], small_m[None], *rec_params,
                                          zero_state, zero_state, zero_tail,
                                          n_pad=n_pad, emit_state=True, cps=META_CHUNKS)

    x2d = x.reshape(bsz * seq, d)
    proj, small = _in_proj(x2d, norm_w, w_bf, w_gate, w_small, lb_logits, tm=min(1024, bsz * seq))
    oa, ob = _recurrence(proj.reshape(bsz, seq, -1), small.reshape(bsz, seq, -1), *rec_params,
                         shg0, sgd0, tail0, n_pad=0, emit_state=False, cps=REC_CHUNKS_PER_STEP)

    h1 = _merge(oa.reshape(bsz * seq, MIX_W), ob.reshape(bsz * seq, MIX_W), proj,
                n_mix // (2 * d), x2d, w_branch_a[layer].astype(BF16), w_branch_b[layer].astype(BF16),
                w_out[layer].astype(BF16), tm=min(256, bsz * seq))
    out = _ffn(h1, w_ffn_in[layer].astype(BF16), w_ffn_out[layer].astype(BF16),
               ffn_norm_w[layer][None, :], final_norm_w[None, :], tm=min(1024, bsz * seq), tf=512)
    return out.reshape(bsz, seq, d)
```

```python
import functools

import numpy as np
import jax
import jax.numpy as jnp
from jax import lax
from jax.experimental import pallas as pl
from jax.experimental.pallas import tpu as pltpu

F32 = jnp.float32
BF16 = jnp.bfloat16

EPS = 1e-6
LOG2_E = 1.4426950408889634
N_META = 16
CHUNK = 64
HEADS = 8
HEAD_DIM = 128
MIX_W = HEADS * HEAD_DIM
REC_CHUNKS_PER_STEP = 4
GDN_GROUP_CHUNKS = 2
META_CHUNKS = 2
CONV_K = 4
CONV_TAIL = 8
LANES = 128
SUBLANES = 8
VMEM_LIMIT = 56 * 1024 * 1024
VMEM_LIMIT_FFN = 60 * 1024 * 1024

IN_PROJ_ROWS = 1024
MERGE_ROWS = 256
FFN_ROWS = 1024
FFN_COLS = 512

LEVELS = tuple(CHUNK >> (i + 1) for i in range(CHUNK.bit_length() - 1))


def _pair_level_table():
    t = np.arange(CHUNK)[:, None]
    j = np.arange(CHUNK)[None, :]
    lvl = np.where(j > t, -1, 0).astype(np.int32)
    for idx, m in enumerate(LEVELS):
        pair = (t // (2 * m) == j // (2 * m)) & (t // m != j // m) & (j < t)
        lvl = np.where(pair, idx + 1, lvl)
    return lvl


def _cumsum_matrix():
    t = np.arange(CHUNK)[:, None]
    i = np.arange(CHUNK)[None, :]
    return np.concatenate([i <= t, i > t], axis=0).astype(np.float32)


def _bdot(a, b):
    return jnp.dot(a, b, preferred_element_type=F32)


def _dot(a, b):
    return _bdot(a.astype(BF16), b.astype(BF16))


def _dot_nt(a, b):
    return lax.dot_general(a.astype(BF16), b.astype(BF16), (((1,), (1,)), ((), ())),
                           preferred_element_type=F32)


def _dot_tn(a, b):
    return lax.dot_general(a.astype(BF16), b.astype(BF16), (((0,), (0,)), ((), ())),
                           preferred_element_type=F32)


def _exact_left_dot(mat01, x, terms):
    n = x.shape[1]
    pieces = []
    r = x
    for _ in range(terms):
        p = r.astype(BF16)
        pieces.append(p)
        r = r - p.astype(F32)
    y = _bdot(mat01, jnp.concatenate(pieces, axis=1))
    out = y[:, 0:n]
    for i in range(1, terms):
        out = out + y[:, i * n:(i + 1) * n]
    return out


def _sigmoid(x):
    return 0.5 * jnp.tanh(0.5 * x) + 0.5


def _silu(x):
    h = 0.5 * x
    return h + h * jnp.tanh(h)


def _softplus(x):
    return jnp.maximum(x, 0.0) + jnp.log(1.0 + jnp.exp(-jnp.abs(x)))


def _rms_norm_rows(x, w):
    return x * lax.rsqrt(jnp.mean(x * x, axis=-1, keepdims=True) + EPS) * w


def _select_by_level(masks, parts):
    out = jnp.where(masks[-1], parts[-1], 0.0)
    for m, p in zip(masks[-2::-1], parts[-2::-1]):
        out = jnp.where(m, p, out)
    return out


TILE_SILU = (0, 3, 7)
TILE_LOG2_FORGET = 1
TILE_PLAIN = (2, 4, 5, 6)
N_MIX_TILES = 8


def _forget_lower_bound(lb_logits):
    ee = jnp.exp(lb_logits - jnp.max(lb_logits, axis=0, keepdims=True))
    return ee[0:1, :] / jnp.sum(ee, axis=0, keepdims=True)


def _in_proj_kernel(x_ref, nw_ref, wm_ref, wg_ref, ws_ref, lbl_ref, o_ref, os_ref, xn_ref):
    j = pl.program_id(1)

    @pl.when(j == 0)
    def _():
        xn_ref[...] = _rms_norm_rows(x_ref[...], nw_ref[...]).astype(BF16)
        os_ref[...] = jnp.dot(xn_ref[...], ws_ref[...], preferred_element_type=F32)

    def tile(w_ref, fn):
        o_ref[...] = fn(jnp.dot(xn_ref[...], w_ref[...], preferred_element_type=F32))

    def is_any(tiles):
        hit = j == tiles[0]
        for t in tiles[1:]:
            hit = hit | (j == t)
        return hit

    @pl.when(is_any(TILE_SILU))
    def _():
        tile(wm_ref, _silu)

    @pl.when(j == TILE_LOG2_FORGET)
    def _():
        lb = _forget_lower_bound(lbl_ref[...])
        tile(wm_ref, lambda a: jnp.log(lb + (1.0 - lb) * _sigmoid(a)) * LOG2_E)

    @pl.when(is_any(TILE_PLAIN))
    def _():
        tile(wm_ref, lambda a: a)

    @pl.when(j >= N_MIX_TILES)
    def _():
        tile(wg_ref, _sigmoid)


def _in_proj(x2d, norm_w, w_mix, w_gate, w_small, lb_logits, tm):
    m, d = x2d.shape
    tn = MIX_W
    n_mix_tiles = N_MIX_TILES
    assert lb_logits.shape[1] == tn and w_mix.shape[1] >= n_mix_tiles * tn
    n = n_mix_tiles * tn + w_gate.shape[1]
    ns = w_small.shape[1]
    return pl.pallas_call(
        _in_proj_kernel,
        grid=(m // tm, n // tn),
        in_specs=[
            pl.BlockSpec((tm, d), lambda i, j: (i, 0)),
            pl.BlockSpec((1, d), lambda i, j: (0, 0)),
            pl.BlockSpec((d, tn), lambda i, j: (0, jnp.minimum(j, n_mix_tiles - 1))),
            pl.BlockSpec((d, tn), lambda i, j: (0, jnp.maximum(j - n_mix_tiles, 0))),
            pl.BlockSpec((d, ns), lambda i, j: (0, 0)),
            pl.BlockSpec(lb_logits.shape, lambda i, j: (0, 0)),
        ],
        out_specs=[
            pl.BlockSpec((tm, tn), lambda i, j: (i, j)),
            pl.BlockSpec((tm, ns), lambda i, j: (i, 0)),
        ],
        out_shape=[jax.ShapeDtypeStruct((m, n), F32), jax.ShapeDtypeStruct((m, ns), F32)],
        scratch_shapes=[pltpu.VMEM((tm, d), BF16)],
        compiler_params=pltpu.CompilerParams(
            dimension_semantics=("arbitrary", "arbitrary"), vmem_limit_bytes=VMEM_LIMIT),
        name="in_proj",
    )(x2d, norm_w, w_mix, w_gate, w_small, lb_logits)


def _unit_lower_inverse_minus_eye(a_list, masks):
    e_list = [-jnp.where(masks[len(LEVELS)], a, 0.0) for a in a_list]
    for idx in range(len(LEVELS) - 2, -1, -1):
        off = [jnp.where(masks[idx + 1], a, 0.0) for a in a_list]
        x_list = [o + _dot(o, e) for o, e in zip(off, e_list)]
        e_list = [e - x - _dot(e, x) for e, x in zip(e_list, x_list)]
    return e_list


def _hgrn2_intra(p_ref, rows, row_valid, cum_mat, masks):
    tasks = []
    for r0, valid in zip(rows, row_valid):
        for h in range(HEADS):
            lo, hi = h * HEAD_DIM, (h + 1) * HEAD_DIM
            q = p_ref[0, r0:r0 + CHUNK, lo:hi]
            log_f = p_ref[0, r0:r0 + CHUNK, MIX_W + lo:MIX_W + hi]
            f = jnp.exp2(log_f)
            k = 1.0 - f
            if valid is not None:
                k = jnp.where(valid, k, 0.0)
                log_f = jnp.where(valid, log_f, 0.0)
                f = jnp.where(valid, f, 1.0)
            v = p_ref[0, r0:r0 + CHUNK, 2 * MIX_W + lo:2 * MIX_W + hi].astype(BF16)
            tasks.append(dict(q=q, f=f, k=k, log_f=log_f, v=v))
    for t in tasks:
        t["b"] = _exact_left_dot(cum_mat, t.pop("log_f"), terms=2)
    nt = lambda x, y: lax.dot_general(x, y, (((1,), (1,)), ((), ())), preferred_element_type=F32)
    for t in tasks:
        q, k, b = t["q"], t["k"], t.pop("b")
        e_cum = jnp.exp2(b)
        e_rest = jnp.exp2(jnp.broadcast_to(b[CHUNK - 1:CHUNK, :], b.shape) - b)
        qb, kb = q.astype(BF16), k.astype(BF16)
        parts = [nt(qb, kb)]
        for m in LEVELS:
            if m == 1:
                parts.append(nt(qb * t["f"].astype(BF16), kb))
            else:
                e_l = jnp.exp2(-jnp.abs(b - _block_midpoint_rows(b, m))).astype(BF16)
                parts.append(nt(qb * e_l, kb * e_l))
        t["scores"] = _select_by_level(masks, parts).astype(BF16)
        t["q_cum"] = (q * e_cum).astype(BF16)
        t["k_rest"] = (k * e_rest).astype(BF16)
        t["e_last"] = e_cum[CHUNK - 1:CHUNK, :]
        del t["q"], t["k"], t["f"]
    return tasks


def _block_midpoint_rows(b, m):
    if 2 * m >= SUBLANES:
        pieces = [jnp.broadcast_to(b[r0 + m - 1:r0 + m, :], (2 * m, b.shape[1])) for r0 in range(0, CHUNK, 2 * m)]
        return jnp.concatenate(pieces, axis=0)
    assert 4 * m == SUBLANES
    lower = lax.broadcasted_iota(jnp.int32, (SUBLANES, b.shape[1]), 0) < 2 * m
    pieces = []
    for r0 in range(0, CHUNK, SUBLANES):
        lo_half = jnp.broadcast_to(b[r0 + m - 1:r0 + m, :], (SUBLANES, b.shape[1]))
        hi_half = jnp.broadcast_to(b[r0 + 3 * m - 1:r0 + 3 * m, :], (SUBLANES, b.shape[1]))
        pieces.append(jnp.where(lower, lo_half, hi_half))
    return jnp.concatenate(pieces, axis=0)


def _gdn_intra(conv, rows, beta_all, gc_all, gc_rows, egc_all, erest_all, masks):
    strict = masks[1]
    for m in masks[2:]:
        strict = strict | m
    incl = strict | masks[0]
    tasks = []
    for si, r0 in enumerate(rows):
        for h in range(HEADS):
            lo, hi = h * HEAD_DIM, (h + 1) * HEAD_DIM
            q = _silu(conv[h][r0:r0 + CHUNK])
            k = _silu(conv[HEADS + h][r0:r0 + CHUNK])
            v = _silu(conv[2 * HEADS + h][r0:r0 + CHUNK])
            q = q * (lax.rsqrt(jnp.sum(q * q, axis=-1, keepdims=True) + EPS) * (HEAD_DIM ** -0.5))
            k = k * lax.rsqrt(jnp.sum(k * k, axis=-1, keepdims=True) + EPS)
            beta_c = beta_all[r0:r0 + CHUNK, h:h + 1]
            gc_c = gc_all[si][:, h:h + 1]
            egc_c = egc_all[si][:, h:h + 1]
            tasks.append(dict(
                q=q, k=k, beta_c=beta_c,
                rel=jnp.exp(jnp.minimum(gc_c - gc_rows[si][h:h + 1, :], 0.0)),
                rhs=jnp.concatenate([(beta_c * egc_c) * k, beta_c * v], axis=1),
                q_cum=(q * egc_c).astype(BF16),
                k_rest=(k * erest_all[si][:, h:h + 1]).astype(BF16),
                e_last=egc_all[si][CHUNK - 1:CHUNK, h:h + 1]))
    for t in tasks:
        qk_kk = _dot_nt(jnp.concatenate([t.pop("q"), t["k"]], axis=0), t.pop("k"))
        rel = t.pop("rel")
        t["attn"] = jnp.where(incl, qk_kk[0:CHUNK] * rel, 0.0).astype(BF16)
        t["a"] = jnp.where(strict, t.pop("beta_c") * qk_kk[CHUNK:2 * CHUNK] * rel, 0.0)
    e_list = _unit_lower_inverse_minus_eye([t.pop("a") for t in tasks], masks)
    for t, e in zip(tasks, e_list):
        rhs = t.pop("rhs")
        wu = rhs + _dot(e, rhs)
        t["w"] = wu[:, 0:HEAD_DIM].astype(BF16)
        t["u"] = wu[:, HEAD_DIM:2 * HEAD_DIM]
    return tasks


def _recurrence_kernel(n_pad, emit_state, cps, *refs):
    (p_ref, sm_ref, hgw_ref, cw_ref, alog_ref, dtb_ref, gdw_ref,
     cum_ref, lvl_ref, shg0_ref, sgd0_ref, tail0_ref) = refs[:12]
    if emit_state:
        oa_ref, ob_ref, shg_out, sgd_out, tail_out, shg_ref, sgd_ref, cbuf_ref = refs[12:]
    else:
        oa_ref, ob_ref, shg_ref, sgd_ref, cbuf_ref = refs[12:]
    c = pl.program_id(1)
    n_rows = cps * CHUNK
    rows = [s * CHUNK for s in range(cps)]

    @pl.when(c == 0)
    def _():
        shg_ref[...] = shg0_ref[...]
        sgd_ref[...] = sgd0_ref[...]
        for blk in range(3 * HEADS):
            cbuf_ref[blk, 0:CONV_TAIL, :] = tail0_ref[:, blk * LANES:(blk + 1) * LANES]

    lvl = lvl_ref[...]
    masks = [lvl == i for i in range(len(LEVELS) + 1)]
    cum_mat = cum_ref[...]
    if n_pad:
        row_ids = lax.broadcasted_iota(jnp.int32, (n_rows, 1), 0)
        valid_all = row_ids >= n_pad
        row_valid = [valid_all[r0:r0 + CHUNK] for r0 in rows]
    else:
        valid_all = None
        row_valid = [None] * cps

    hg = _hgrn2_intra(p_ref, rows, row_valid, cum_mat[0:CHUNK], masks)

    gd0 = 4 * MIX_W
    cw = cw_ref[...]
    first = CONV_TAIL - (CONV_K - 1)
    conv = []
    for blk in range(3 * HEADS):
        lo, hi = blk * LANES, (blk + 1) * LANES
        cbuf_ref[blk, CONV_TAIL:CONV_TAIL + n_rows, :] = p_ref[0, :, gd0 + lo:gd0 + hi]
        acc = cw[0:1, lo:hi] * cbuf_ref[blk, first:first + n_rows, :]
        for tap in range(1, CONV_K):
            acc = acc + cw[tap:tap + 1, lo:hi] * cbuf_ref[blk, first + tap:first + tap + n_rows, :]
        conv.append(acc)
        cbuf_ref[blk, 0:CONV_TAIL, :] = cbuf_ref[blk, n_rows:n_rows + CONV_TAIL, :]

    beta_all = _sigmoid(sm_ref[0, :, LANES:2 * LANES])
    g_all = -jnp.exp(alog_ref[...]) * _softplus(sm_ref[0, :, 0:LANES] + dtb_ref[...])
    if valid_all is not None:
        beta_all = jnp.where(valid_all, beta_all, 0.0)
        g_all = jnp.where(valid_all, g_all, 0.0)
    gsum = [_exact_left_dot(cum_mat, g_all[r0:r0 + CHUNK], terms=3) for r0 in rows]
    gc_all = [g[0:CHUNK] for g in gsum]
    gc_rows = [g.T for g in gc_all]
    egc_all = [jnp.exp(g) for g in gc_all]
    erest_all = [jnp.exp(g[CHUNK:2 * CHUNK]) for g in gsum]

    gd = []
    for g0 in range(0, cps, GDN_GROUP_CHUNKS):
        sl = slice(g0, g0 + GDN_GROUP_CHUNKS)
        gd += _gdn_intra(conv, rows[sl], beta_all, gc_all[sl], gc_rows[sl], egc_all[sl], erest_all[sl], masks)

    s_hg = [shg_ref[h] for h in range(HEADS)]
    s_gd = [sgd_ref[h] for h in range(HEADS)]
    hgw = hgw_ref[...]
    gdw = gdw_ref[...]
    for si, r0 in enumerate(rows):
        hg_t = hg[si * HEADS:(si + 1) * HEADS]
        gd_t = gd[si * HEADS:(si + 1) * HEADS]
        o_hg = [lax.dot_general(t["q_cum"], s.astype(BF16), (((1,), (1,)), ((), ())),
                                preferred_element_type=F32) + _bdot(t["scores"], t["v"])
                for t, s in zip(hg_t, s_hg)]
        s_hg = [s * t["e_last"] + _dot_tn(t["v"], t["k_rest"]) for t, s in zip(hg_t, s_hg)]
        wq_s = [_bdot(jnp.concatenate([t["w"], t["q_cum"]], axis=0), s.astype(BF16))
                for t, s in zip(gd_t, s_gd)]
        v_new = [(t["u"] - ws[0:CHUNK]).astype(BF16) for t, ws in zip(gd_t, wq_s)]
        o_gd = [ws[CHUNK:2 * CHUNK] + _bdot(t["attn"], vn) for t, ws, vn in zip(gd_t, wq_s, v_new)]
        s_gd = [s * t["e_last"] + _dot_tn(t["k_rest"], vn) for t, s, vn in zip(gd_t, s_gd, v_new)]
        for h in range(HEADS):
            lo, hi = h * HEAD_DIM, (h + 1) * HEAD_DIM
            gate = p_ref[0, r0:r0 + CHUNK, 3 * MIX_W + lo:3 * MIX_W + hi]
            oa_ref[0, r0:r0 + CHUNK, lo:hi] = (_rms_norm_rows(o_hg[h], hgw) * gate).astype(oa_ref.dtype)
            gate = p_ref[0, r0:r0 + CHUNK, gd0 + 3 * MIX_W + lo:gd0 + 3 * MIX_W + hi]
            ob_ref[0, r0:r0 + CHUNK, lo:hi] = (_rms_norm_rows(o_gd[h], gdw) * gate).astype(ob_ref.dtype)
    for h in range(HEADS):
        shg_ref[h] = s_hg[h]
        sgd_ref[h] = s_gd[h]

    if emit_state:
        @pl.when(c == pl.num_programs(1) - 1)
        def _():
            shg_out[...] = shg_ref[...]
            sgd_out[...] = sgd_ref[...]
            for blk in range(3 * HEADS):
                tail_out[:, blk * LANES:(blk + 1) * LANES] = cbuf_ref[blk, 0:CONV_TAIL, :]


def _recurrence(proj, small, hg_norm_w, conv_w, a_log, dt_bias, gd_norm_w,
                shg0, sgd0, tail0, n_pad, emit_state, cps):
    bsz, length, _ = proj.shape
    n_rows = cps * CHUNK
    cum = jnp.asarray(_cumsum_matrix(), BF16)
    lvl = jnp.asarray(_pair_level_table())
    const = lambda *shape: pl.BlockSpec(shape, lambda b, c: (0,) * len(shape))
    state_shape = (HEADS, HEAD_DIM, HEAD_DIM)
    in_specs = [
        pl.BlockSpec((1, n_rows, 8 * MIX_W), lambda b, c: (b, c, 0)),
        pl.BlockSpec((1, n_rows, 2 * LANES), lambda b, c: (b, c, 0)),
        const(1, HEAD_DIM), const(CONV_K, 3 * MIX_W),
        const(1, LANES), const(1, LANES), const(1, HEAD_DIM),
        const(*cum.shape), const(CHUNK, CHUNK),
        const(*state_shape), const(*state_shape), const(CONV_TAIL, 3 * MIX_W),
    ]
    out_specs = [
        pl.BlockSpec((1, n_rows, MIX_W), lambda b, c: (b, c, 0)),
        pl.BlockSpec((1, n_rows, MIX_W), lambda b, c: (b, c, 0)),
    ]
    out_shape = [jax.ShapeDtypeStruct((bsz, length, MIX_W), BF16),
                 jax.ShapeDtypeStruct((bsz, length, MIX_W), BF16)]
    if emit_state:
        out_specs += [const(*state_shape), const(*state_shape), const(CONV_TAIL, 3 * MIX_W)]
        out_shape += [jax.ShapeDtypeStruct(state_shape, F32), jax.ShapeDtypeStruct(state_shape, F32),
                      jax.ShapeDtypeStruct((CONV_TAIL, 3 * MIX_W), F32)]
    return pl.pallas_call(
        functools.partial(_recurrence_kernel, n_pad, emit_state, cps),
        grid=(bsz, length // n_rows),
        in_specs=in_specs,
        out_specs=out_specs,
        out_shape=out_shape,
        scratch_shapes=[pltpu.VMEM(state_shape, F32), pltpu.VMEM(state_shape, F32),
                        pltpu.VMEM((3 * HEADS, CONV_TAIL + n_rows, LANES), F32)],
        compiler_params=pltpu.CompilerParams(
            dimension_semantics=("arbitrary", "arbitrary"), vmem_limit_bytes=VMEM_LIMIT),
        name="recurrence_meta" if emit_state else "recurrence",
    )(proj, small, hg_norm_w, conv_w, a_log, dt_bias, gd_norm_w,
      cum, lvl, shg0, sgd0, tail0)


def _merge_kernel(oa_ref, ob_ref, g_ref, x_ref, wa_ref, wb_ref, wo_ref, h_ref):
    d = x_ref.shape[1]
    ya = jnp.dot(oa_ref[...], wa_ref[...], preferred_element_type=F32)
    yb = jnp.dot(ob_ref[...], wb_ref[...], preferred_element_type=F32)
    merged = g_ref[:, 0:d] * ya + g_ref[:, d:2 * d] * yb
    h_ref[...] = x_ref[...] + jnp.dot(merged.astype(BF16), wo_ref[...], preferred_element_type=F32)


def _merge(oa, ob, proj, gate_block, x2d, wa, wb, wo, tm):
    m, d = x2d.shape
    single = dict(pipeline_mode=pl.Buffered(1))
    return pl.pallas_call(
        _merge_kernel,
        grid=(m // tm,),
        in_specs=[
            pl.BlockSpec((tm, MIX_W), lambda i: (i, 0)),
            pl.BlockSpec((tm, MIX_W), lambda i: (i, 0)),
            pl.BlockSpec((tm, 2 * d), lambda i: (i, gate_block)),
            pl.BlockSpec((tm, d), lambda i: (i, 0)),
            pl.BlockSpec((MIX_W, d), lambda i: (0, 0), **single),
            pl.BlockSpec((MIX_W, d), lambda i: (0, 0), **single),
            pl.BlockSpec((d, d), lambda i: (0, 0), **single),
        ],
        out_specs=pl.BlockSpec((tm, d), lambda i: (i, 0)),
        out_shape=jax.ShapeDtypeStruct((m, d), F32),
        compiler_params=pltpu.CompilerParams(
            dimension_semantics=("arbitrary",), vmem_limit_bytes=VMEM_LIMIT),
        name="merge",
    )(oa, ob, proj, x2d, wa, wb, wo)


def _ffn_kernel(h_ref, wg_ref, wu_ref, wd_ref, nw_ref, fw_ref, o_ref, xn_ref, hid_ref):
    j = pl.program_id(1)
    nf = pl.num_programs(1) - 1

    def gate_up(slot):
        xn = xn_ref[...]
        gate = jnp.dot(xn, wg_ref[...], preferred_element_type=F32)
        up = jnp.dot(xn, wu_ref[...], preferred_element_type=F32)
        hid_ref[slot] = (_silu(gate) * up).astype(BF16)

    def down(slot):
        o_ref[...] += jnp.dot(hid_ref[slot], wd_ref[...], preferred_element_type=F32)

    @pl.when(j == 0)
    def _():
        h = h_ref[...]
        o_ref[...] = h
        xn_ref[...] = _rms_norm_rows(h, nw_ref[...]).astype(BF16)
        gate_up(0)

    @pl.when((j > 0) & (j < nf))
    def _():
        gate_up(j % 2)
        down((j - 1) % 2)

    @pl.when(j == nf)
    def _():
        down((j - 1) % 2)
        o_ref[...] = _rms_norm_rows(o_ref[...], fw_ref[...])


def _ffn(h, w_gate_up, w_down, ffn_norm_w, final_norm_w, tm, tf):
    m, d = h.shape
    ff = w_down.shape[0]
    nf = ff // tf
    return pl.pallas_call(
        _ffn_kernel,
        grid=(m // tm, nf + 1),
        in_specs=[
            pl.BlockSpec((tm, d), lambda i, j: (i, 0)),
            pl.BlockSpec((d, tf), lambda i, j: (0, jnp.minimum(j, nf - 1))),
            pl.BlockSpec((d, tf), lambda i, j: (0, nf + jnp.minimum(j, nf - 1))),
            pl.BlockSpec((tf, d), lambda i, j: (jnp.maximum(j - 1, 0), 0)),
            pl.BlockSpec((1, d), lambda i, j: (0, 0)),
            pl.BlockSpec((1, d), lambda i, j: (0, 0)),
        ],
        out_specs=pl.BlockSpec((tm, d), lambda i, j: (i, 0)),
        out_shape=jax.ShapeDtypeStruct((m, d), F32),
        scratch_shapes=[pltpu.VMEM((tm, d), BF16), pltpu.VMEM((2, tm, tf), BF16)],
        compiler_params=pltpu.CompilerParams(
            dimension_semantics=("arbitrary", "arbitrary"), vmem_limit_bytes=VMEM_LIMIT_FFN),
        name="ffn",
    )(h, w_gate_up, w_gate_up, w_down, ffn_norm_w, final_norm_w)


def _pad_lanes(v):
    return jnp.pad(v.astype(F32), (0, LANES - v.shape[0]))[None, :]


def kernel(x, meta_tokens, lb_logits, mix_norm_w, w_in, hg_norm_w, gd_conv_w, gd_a_log, gd_dt_bias,
           gd_norm_w, w_branch_a, w_branch_b, w_out, ffn_norm_w, w_ffn_in, w_ffn_out, final_norm_w):
    bsz, seq, d = x.shape
    layer = 0
    assert mix_norm_w.shape[0] == 1, "single-layer block"
    assert seq % (REC_CHUNKS_PER_STEP * CHUNK) == 0 and meta_tokens.shape[0] == N_META <= CHUNK

    w = w_in[layer]
    n_mix = 8 * MIX_W
    assert n_mix == N_MIX_TILES * MIX_W
    w_bf = w.astype(BF16)
    w_gate = w_bf[:, n_mix + 2 * HEADS:]
    pad = jnp.zeros((d, LANES - HEADS), BF16)
    w_small = jnp.concatenate([w_bf[:, n_mix:n_mix + HEADS], pad,
                               w_bf[:, n_mix + HEADS:n_mix + 2 * HEADS], pad], axis=1)
    norm_w = mix_norm_w[layer][None, :]
    lb_logits = lb_logits.astype(F32)
    rec_params = (hg_norm_w[layer][None, :], gd_conv_w[layer],
                  _pad_lanes(gd_a_log[layer]), _pad_lanes(gd_dt_bias[layer]), gd_norm_w[layer][None, :])

    n_pad = META_CHUNKS * CHUNK - N_META
    x_meta = jnp.concatenate([jnp.zeros((n_pad, d), x.dtype), meta_tokens.astype(x.dtype)], axis=0)
    proj_m, small_m = _in_proj(x_meta, norm_w, w_bf, w_gate, w_small, lb_logits, tm=META_CHUNKS * CHUNK)
    zero_state = jnp.zeros((HEADS, HEAD_DIM, HEAD_DIM), F32)
    zero_tail = jnp.zeros((CONV_TAIL, 3 * MIX_W), F32)
    _, _, shg0, sgd0, tail0 = _recurrence(proj_m[None], small_m[None], *rec_params,
                                          zero_state, zero_state, zero_tail,
                                          n_pad=n_pad, emit_state=True, cps=META_CHUNKS)

    x2d = x.reshape(bsz * seq, d)
    proj, small = _in_proj(x2d, norm_w, w_bf, w_gate, w_small, lb_logits, tm=min(IN_PROJ_ROWS, bsz * seq))
    oa, ob = _recurrence(proj.reshape(bsz, seq, -1), small.reshape(bsz, seq, -1), *rec_params,
                         shg0, sgd0, tail0, n_pad=0, emit_state=False, cps=REC_CHUNKS_PER_STEP)

    h1 = _merge(oa.reshape(bsz * seq, MIX_W), ob.reshape(bsz * seq, MIX_W), proj,
                n_mix // (2 * d), x2d, w_branch_a[layer].astype(BF16), w_branch_b[layer].astype(BF16),
                w_out[layer].astype(BF16), tm=min(MERGE_ROWS, bsz * seq))
    out = _ffn(h1, w_ffn_in[layer].astype(BF16), w_ffn_out[layer].astype(BF16),
               ffn_norm_w[layer][None, :], final_norm_w[None, :], tm=min(FFN_ROWS, bsz * seq), tf=FFN_COLS)
    return out.reshape(bsz, seq, d)
```

```python
import functools

import numpy as np
import jax
import jax.numpy as jnp
from jax import lax
from jax.experimental import pallas as pl
from jax.experimental.pallas import tpu as pltpu

F32 = jnp.float32
BF16 = jnp.bfloat16

EPS = 1e-6
LOG2_E = 1.4426950408889634
N_META = 16
CHUNK = 64
HEADS = 8
HEAD_DIM = 128
MIX_W = HEADS * HEAD_DIM
REC_CHUNKS_PER_STEP = 4
GDN_GROUP_CHUNKS = 2
META_CHUNKS = 2
HGRN2_LAG = 8
CONV_K = 4
CONV_TAIL = 8
LANES = 128
SUBLANES = 8
VMEM_LIMIT = 56 * 1024 * 1024
VMEM_LIMIT_FFN = 60 * 1024 * 1024

IN_PROJ_ROWS = 1024
MERGE_ROWS = 256
FFN_ROWS = 1024
FFN_COLS = 512

LEVELS = tuple(CHUNK >> (i + 1) for i in range(CHUNK.bit_length() - 1))


def _pair_level_table():
    t = np.arange(CHUNK)[:, None]
    j = np.arange(CHUNK)[None, :]
    lvl = np.where(j > t, -1, 0).astype(np.int32)
    for idx, m in enumerate(LEVELS):
        pair = (t // (2 * m) == j // (2 * m)) & (t // m != j // m) & (j < t)
        lvl = np.where(pair, idx + 1, lvl)
    return lvl


def _cumsum_matrix():
    t = np.arange(CHUNK)[:, None]
    i = np.arange(CHUNK)[None, :]
    return np.concatenate([i <= t, i > t], axis=0).astype(np.float32)


def _bdot(a, b):
    return jnp.dot(a, b, preferred_element_type=F32)


def _dot(a, b):
    return _bdot(a.astype(BF16), b.astype(BF16))


def _dot_nt(a, b):
    return lax.dot_general(a.astype(BF16), b.astype(BF16), (((1,), (1,)), ((), ())),
                           preferred_element_type=F32)


def _dot_tn(a, b):
    return lax.dot_general(a.astype(BF16), b.astype(BF16), (((0,), (0,)), ((), ())),
                           preferred_element_type=F32)


def _exact_left_dot(mat01, x, terms):
    n = x.shape[1]
    pieces = []
    r = x
    for _ in range(terms):
        p = r.astype(BF16)
        pieces.append(p)
        r = r - p.astype(F32)
    y = _bdot(mat01, jnp.concatenate(pieces, axis=1))
    out = y[:, 0:n]
    for i in range(1, terms):
        out = out + y[:, i * n:(i + 1) * n]
    return out


def _sigmoid(x):
    return 0.5 * jnp.tanh(0.5 * x) + 0.5


def _silu(x):
    h = 0.5 * x
    return h + h * jnp.tanh(h)


def _softplus(x):
    return jnp.maximum(x, 0.0) + jnp.log(1.0 + jnp.exp(-jnp.abs(x)))


def _rms_norm_rows(x, w):
    return x * lax.rsqrt(jnp.mean(x * x, axis=-1, keepdims=True) + EPS) * w


def _select_by_level(masks, parts):
    out = jnp.where(masks[-1], parts[-1], 0.0)
    for m, p in zip(masks[-2::-1], parts[-2::-1]):
        out = jnp.where(m, p, out)
    return out


TILE_SILU = (0, 3, 7)
TILE_LOG2_FORGET = 1
TILE_PLAIN = (2, 4, 5, 6)
N_MIX_TILES = 8


def _forget_lower_bound(lb_logits):
    ee = jnp.exp(lb_logits - jnp.max(lb_logits, axis=0, keepdims=True))
    return ee[0:1, :] / jnp.sum(ee, axis=0, keepdims=True)


def _in_proj_kernel(x_ref, nw_ref, wm_ref, wg_ref, ws_ref, lbl_ref, o_ref, os_ref, xn_ref):
    j = pl.program_id(1)

    @pl.when(j == 0)
    def _():
        xn_ref[...] = _rms_norm_rows(x_ref[...], nw_ref[...]).astype(BF16)
        os_ref[...] = jnp.dot(xn_ref[...], ws_ref[...], preferred_element_type=F32)

    def tile(w_ref, fn):
        o_ref[...] = fn(jnp.dot(xn_ref[...], w_ref[...], preferred_element_type=F32))

    def is_any(tiles):
        hit = j == tiles[0]
        for t in tiles[1:]:
            hit = hit | (j == t)
        return hit

    @pl.when(is_any(TILE_SILU))
    def _():
        tile(wm_ref, _silu)

    @pl.when(j == TILE_LOG2_FORGET)
    def _():
        lb = _forget_lower_bound(lbl_ref[...])
        tile(wm_ref, lambda a: jnp.log(lb + (1.0 - lb) * _sigmoid(a)) * LOG2_E)

    @pl.when(is_any(TILE_PLAIN))
    def _():
        tile(wm_ref, lambda a: a)

    @pl.when(j >= N_MIX_TILES)
    def _():
        tile(wg_ref, _sigmoid)


def _in_proj(x2d, norm_w, w_mix, w_gate, w_small, lb_logits, tm):
    m, d = x2d.shape
    tn = MIX_W
    n_mix_tiles = N_MIX_TILES
    assert lb_logits.shape[1] == tn and w_mix.shape[1] >= n_mix_tiles * tn
    n = n_mix_tiles * tn + w_gate.shape[1]
    ns = w_small.shape[1]
    return pl.pallas_call(
        _in_proj_kernel,
        grid=(m // tm, n // tn),
        in_specs=[
            pl.BlockSpec((tm, d), lambda i, j: (i, 0)),
            pl.BlockSpec((1, d), lambda i, j: (0, 0)),
            pl.BlockSpec((d, tn), lambda i, j: (0, jnp.minimum(j, n_mix_tiles - 1))),
            pl.BlockSpec((d, tn), lambda i, j: (0, jnp.maximum(j - n_mix_tiles, 0))),
            pl.BlockSpec((d, ns), lambda i, j: (0, 0)),
            pl.BlockSpec(lb_logits.shape, lambda i, j: (0, 0)),
        ],
        out_specs=[
            pl.BlockSpec((tm, tn), lambda i, j: (i, j)),
            pl.BlockSpec((tm, ns), lambda i, j: (i, 0)),
        ],
        out_shape=[jax.ShapeDtypeStruct((m, n), F32), jax.ShapeDtypeStruct((m, ns), F32)],
        scratch_shapes=[pltpu.VMEM((tm, d), BF16)],
        compiler_params=pltpu.CompilerParams(
            dimension_semantics=("arbitrary", "arbitrary"), vmem_limit_bytes=VMEM_LIMIT),
        name="in_proj",
    )(x2d, norm_w, w_mix, w_gate, w_small, lb_logits)


def _unit_lower_inverse_minus_eye(a_list, masks):
    e_list = [-jnp.where(masks[len(LEVELS)], a, 0.0) for a in a_list]
    for idx in range(len(LEVELS) - 2, -1, -1):
        off = [jnp.where(masks[idx + 1], a, 0.0) for a in a_list]
        x_list = [o + _dot(o, e) for o, e in zip(off, e_list)]
        e_list = [e - x - _dot(e, x) for e, x in zip(e_list, x_list)]
    return e_list


def _hgrn2_intra(p_ref, rows, row_valid, cum_mat, masks):
    todo = [(r0, valid, h) for r0, valid in zip(rows, row_valid) for h in range(HEADS)]
    tasks = []
    nt = lambda x, y: lax.dot_general(x, y, (((1,), (1,)), ((), ())), preferred_element_type=F32)

    def start(r0, valid, h):
        lo, hi = h * HEAD_DIM, (h + 1) * HEAD_DIM
        q = p_ref[0, r0:r0 + CHUNK, lo:hi]
        log_f = p_ref[0, r0:r0 + CHUNK, MIX_W + lo:MIX_W + hi]
        f = jnp.exp2(log_f)
        k = 1.0 - f
        if valid is not None:
            k = jnp.where(valid, k, 0.0)
            log_f = jnp.where(valid, log_f, 0.0)
            f = jnp.where(valid, f, 1.0)
        v = p_ref[0, r0:r0 + CHUNK, 2 * MIX_W + lo:2 * MIX_W + hi].astype(BF16)
        b = _exact_left_dot(cum_mat, log_f, terms=2)
        tasks.append(dict(q=q, f=f, k=k, b=b, v=v))

    def finish(t):
        q, k, b = t["q"], t["k"], t.pop("b")
        e_cum = jnp.exp2(b)
        e_rest = jnp.exp2(jnp.broadcast_to(b[CHUNK - 1:CHUNK, :], b.shape) - b)
        qb, kb = q.astype(BF16), k.astype(BF16)
        parts = [nt(qb, kb)]
        for m in LEVELS:
            if m == 1:
                parts.append(nt(qb * t["f"].astype(BF16), kb))
            else:
                e_l = jnp.exp2(-jnp.abs(b - _block_midpoint_rows(b, m))).astype(BF16)
                parts.append(nt(qb * e_l, kb * e_l))
        t["scores"] = _select_by_level(masks, parts).astype(BF16)
        t["q_cum"] = (q * e_cum).astype(BF16)
        t["k_rest"] = (k * e_rest).astype(BF16)
        t["e_last"] = e_cum[CHUNK - 1:CHUNK, :]
        del t["q"], t["k"], t["f"]

    for i in range(len(todo) + HGRN2_LAG):
        if i < len(todo):
            start(*todo[i])
        if i >= HGRN2_LAG:
            finish(tasks[i - HGRN2_LAG])
    return tasks


def _block_midpoint_rows(b, m):
    if 2 * m >= SUBLANES:
        pieces = [jnp.broadcast_to(b[r0 + m - 1:r0 + m, :], (2 * m, b.shape[1])) for r0 in range(0, CHUNK, 2 * m)]
        return jnp.concatenate(pieces, axis=0)
    assert 4 * m == SUBLANES
    lower = lax.broadcasted_iota(jnp.int32, (SUBLANES, b.shape[1]), 0) < 2 * m
    pieces = []
    for r0 in range(0, CHUNK, SUBLANES):
        lo_half = jnp.broadcast_to(b[r0 + m - 1:r0 + m, :], (SUBLANES, b.shape[1]))
        hi_half = jnp.broadcast_to(b[r0 + 3 * m - 1:r0 + 3 * m, :], (SUBLANES, b.shape[1]))
        pieces.append(jnp.where(lower, lo_half, hi_half))
    return jnp.concatenate(pieces, axis=0)


def _gdn_intra(conv, rows, beta_all, gc_all, gc_rows, egc_all, erest_all, masks):
    strict = masks[1]
    for m in masks[2:]:
        strict = strict | m
    incl = strict | masks[0]
    tasks = []
    for si, r0 in enumerate(rows):
        for h in range(HEADS):
            lo, hi = h * HEAD_DIM, (h + 1) * HEAD_DIM
            q = _silu(conv[h][r0:r0 + CHUNK])
            k = _silu(conv[HEADS + h][r0:r0 + CHUNK])
            v = _silu(conv[2 * HEADS + h][r0:r0 + CHUNK])
            q = q * (lax.rsqrt(jnp.sum(q * q, axis=-1, keepdims=True) + EPS) * (HEAD_DIM ** -0.5))
            k = k * lax.rsqrt(jnp.sum(k * k, axis=-1, keepdims=True) + EPS)
            beta_c = beta_all[r0:r0 + CHUNK, h:h + 1]
            gc_c = gc_all[si][:, h:h + 1]
            egc_c = egc_all[si][:, h:h + 1]
            tasks.append(dict(
                q=q, k=k, beta_c=beta_c,
                rel=jnp.exp(jnp.minimum(gc_c - gc_rows[si][h:h + 1, :], 0.0)),
                rhs=jnp.concatenate([(beta_c * egc_c) * k, beta_c * v], axis=1),
                q_cum=(q * egc_c).astype(BF16),
                k_rest=(k * erest_all[si][:, h:h + 1]).astype(BF16),
                e_last=egc_all[si][CHUNK - 1:CHUNK, h:h + 1]))
    for t in tasks:
        qk_kk = _dot_nt(jnp.concatenate([t.pop("q"), t["k"]], axis=0), t.pop("k"))
        rel = t.pop("rel")
        t["attn"] = jnp.where(incl, qk_kk[0:CHUNK] * rel, 0.0).astype(BF16)
        t["a"] = jnp.where(strict, t.pop("beta_c") * qk_kk[CHUNK:2 * CHUNK] * rel, 0.0)
    e_list = _unit_lower_inverse_minus_eye([t.pop("a") for t in tasks], masks)
    for t, e in zip(tasks, e_list):
        rhs = t.pop("rhs")
        wu = rhs + _dot(e, rhs)
        t["w"] = wu[:, 0:HEAD_DIM].astype(BF16)
        t["u"] = wu[:, HEAD_DIM:2 * HEAD_DIM]
    return tasks


def _recurrence_kernel(n_pad, emit_state, cps, *refs):
    (p_ref, sm_ref, hgw_ref, cw_ref, alog_ref, dtb_ref, gdw_ref,
     cum_ref, lvl_ref, shg0_ref, sgd0_ref, tail0_ref) = refs[:12]
    if emit_state:
        oa_ref, ob_ref, shg_out, sgd_out, tail_out, shg_ref, sgd_ref, cbuf_ref = refs[12:]
    else:
        oa_ref, ob_ref, shg_ref, sgd_ref, cbuf_ref = refs[12:]
    c = pl.program_id(1)
    n_rows = cps * CHUNK
    rows = [s * CHUNK for s in range(cps)]

    @pl.when(c == 0)
    def _():
        shg_ref[...] = shg0_ref[...]
        sgd_ref[...] = sgd0_ref[...]
        for blk in range(3 * HEADS):
            cbuf_ref[blk, 0:CONV_TAIL, :] = tail0_ref[:, blk * LANES:(blk + 1) * LANES]

    lvl = lvl_ref[...]
    masks = [lvl == i for i in range(len(LEVELS) + 1)]
    cum_mat = cum_ref[...]
    if n_pad:
        row_ids = lax.broadcasted_iota(jnp.int32, (n_rows, 1), 0)
        valid_all = row_ids >= n_pad
        row_valid = [valid_all[r0:r0 + CHUNK] for r0 in rows]
    else:
        valid_all = None
        row_valid = [None] * cps

    hg = _hgrn2_intra(p_ref, rows, row_valid, cum_mat[0:CHUNK], masks)

    gd0 = 4 * MIX_W
    cw = cw_ref[...]
    first = CONV_TAIL - (CONV_K - 1)
    conv = []
    for blk in range(3 * HEADS):
        lo, hi = blk * LANES, (blk + 1) * LANES
        cbuf_ref[blk, CONV_TAIL:CONV_TAIL + n_rows, :] = p_ref[0, :, gd0 + lo:gd0 + hi]
        acc = cw[0:1, lo:hi] * cbuf_ref[blk, first:first + n_rows, :]
        for tap in range(1, CONV_K):
            acc = acc + cw[tap:tap + 1, lo:hi] * cbuf_ref[blk, first + tap:first + tap + n_rows, :]
        conv.append(acc)
        cbuf_ref[blk, 0:CONV_TAIL, :] = cbuf_ref[blk, n_rows:n_rows + CONV_TAIL, :]

    beta_all = _sigmoid(sm_ref[0, :, LANES:2 * LANES])
    g_all = -jnp.exp(alog_ref[...]) * _softplus(sm_ref[0, :, 0:LANES] + dtb_ref[...])
    if valid_all is not None:
        beta_all = jnp.where(valid_all, beta_all, 0.0)
        g_all = jnp.where(valid_all, g_all, 0.0)
    gsum = [_exact_left_dot(cum_mat, g_all[r0:r0 + CHUNK], terms=3) for r0 in rows]
    gc_all = [g[0:CHUNK] for g in gsum]
    gc_rows = [g.T for g in gc_all]
    egc_all = [jnp.exp(g) for g in gc_all]
    erest_all = [jnp.exp(g[CHUNK:2 * CHUNK]) for g in gsum]

    gd = []
    for g0 in range(0, cps, GDN_GROUP_CHUNKS):
        sl = slice(g0, g0 + GDN_GROUP_CHUNKS)
        gd += _gdn_intra(conv, rows[sl], beta_all, gc_all[sl], gc_rows[sl], egc_all[sl], erest_all[sl], masks)

    s_hg = [shg_ref[h] for h in range(HEADS)]
    s_gd = [sgd_ref[h] for h in range(HEADS)]
    hgw = hgw_ref[...]
    gdw = gdw_ref[...]
    for si, r0 in enumerate(rows):
        hg_t = hg[si * HEADS:(si + 1) * HEADS]
        gd_t = gd[si * HEADS:(si + 1) * HEADS]
        o_hg = [lax.dot_general(t["q_cum"], s.astype(BF16), (((1,), (1,)), ((), ())),
                                preferred_element_type=F32) + _bdot(t["scores"], t["v"])
                for t, s in zip(hg_t, s_hg)]
        s_hg = [s * t["e_last"] + _dot_tn(t["v"], t["k_rest"]) for t, s in zip(hg_t, s_hg)]
        wq_s = [_bdot(jnp.concatenate([t["w"], t["q_cum"]], axis=0), s.astype(BF16))
                for t, s in zip(gd_t, s_gd)]
        v_new = [(t["u"] - ws[0:CHUNK]).astype(BF16) for t, ws in zip(gd_t, wq_s)]
        o_gd = [ws[CHUNK:2 * CHUNK] + _bdot(t["attn"], vn) for t, ws, vn in zip(gd_t, wq_s, v_new)]
        s_gd = [s * t["e_last"] + _dot_tn(t["k_rest"], vn) for t, s, vn in zip(gd_t, s_gd, v_new)]
        for h in range(HEADS):
            lo, hi = h * HEAD_DIM, (h + 1) * HEAD_DIM
            gate = p_ref[0, r0:r0 + CHUNK, 3 * MIX_W + lo:3 * MIX_W + hi]
            oa_ref[0, r0:r0 + CHUNK, lo:hi] = (_rms_norm_rows(o_hg[h], hgw) * gate).astype(oa_ref.dtype)
            gate = p_ref[0, r0:r0 + CHUNK, gd0 + 3 * MIX_W + lo:gd0 + 3 * MIX_W + hi]
            ob_ref[0, r0:r0 + CHUNK, lo:hi] = (_rms_norm_rows(o_gd[h], gdw) * gate).astype(ob_ref.dtype)
    for h in range(HEADS):
        shg_ref[h] = s_hg[h]
        sgd_ref[h] = s_gd[h]

    if emit_state:
        @pl.when(c == pl.num_programs(1) - 1)
        def _():
            shg_out[...] = shg_ref[...]
            sgd_out[...] = sgd_ref[...]
            for blk in range(3 * HEADS):
                tail_out[:, blk * LANES:(blk + 1) * LANES] = cbuf_ref[blk, 0:CONV_TAIL, :]


def _recurrence(proj, small, hg_norm_w, conv_w, a_log, dt_bias, gd_norm_w,
                shg0, sgd0, tail0, n_pad, emit_state, cps):
    bsz, length, _ = proj.shape
    n_rows = cps * CHUNK
    cum = jnp.asarray(_cumsum_matrix(), BF16)
    lvl = jnp.asarray(_pair_level_table())
    const = lambda *shape: pl.BlockSpec(shape, lambda b, c: (0,) * len(shape))
    state_shape = (HEADS, HEAD_DIM, HEAD_DIM)
    in_specs = [
        pl.BlockSpec((1, n_rows, 8 * MIX_W), lambda b, c: (b, c, 0)),
        pl.BlockSpec((1, n_rows, 2 * LANES), lambda b, c: (b, c, 0)),
        const(1, HEAD_DIM), const(CONV_K, 3 * MIX_W),
        const(1, LANES), const(1, LANES), const(1, HEAD_DIM),
        const(*cum.shape), const(CHUNK, CHUNK),
        const(*state_shape), const(*state_shape), const(CONV_TAIL, 3 * MIX_W),
    ]
    out_specs = [
        pl.BlockSpec((1, n_rows, MIX_W), lambda b, c: (b, c, 0)),
        pl.BlockSpec((1, n_rows, MIX_W), lambda b, c: (b, c, 0)),
    ]
    out_shape = [jax.ShapeDtypeStruct((bsz, length, MIX_W), BF16),
                 jax.ShapeDtypeStruct((bsz, length, MIX_W), BF16)]
    if emit_state:
        out_specs += [const(*state_shape), const(*state_shape), const(CONV_TAIL, 3 * MIX_W)]
        out_shape += [jax.ShapeDtypeStruct(state_shape, F32), jax.ShapeDtypeStruct(state_shape, F32),
                      jax.ShapeDtypeStruct((CONV_TAIL, 3 * MIX_W), F32)]
    return pl.pallas_call(
        functools.partial(_recurrence_kernel, n_pad, emit_state, cps),
        grid=(bsz, length // n_rows),
        in_specs=in_specs,
        out_specs=out_specs,
        out_shape=out_shape,
        scratch_shapes=[pltpu.VMEM(state_shape, F32), pltpu.VMEM(state_shape, F32),
                        pltpu.VMEM((3 * HEADS, CONV_TAIL + n_rows, LANES), F32)],
        compiler_params=pltpu.CompilerParams(
            dimension_semantics=("arbitrary", "arbitrary"), vmem_limit_bytes=VMEM_LIMIT),
        name="recurrence_meta" if emit_state else "recurrence",
    )(proj, small, hg_norm_w, conv_w, a_log, dt_bias, gd_norm_w,
      cum, lvl, shg0, sgd0, tail0)


def _merge_kernel(oa_ref, ob_ref, g_ref, x_ref, wa_ref, wb_ref, wo_ref, h_ref):
    d = x_ref.shape[1]
    ya = jnp.dot(oa_ref[...], wa_ref[...], preferred_element_type=F32)
    yb = jnp.dot(ob_ref[...], wb_ref[...], preferred_element_type=F32)
    merged = g_ref[:, 0:d] * ya + g_ref[:, d:2 * d] * yb
    h_ref[...] = x_ref[...] + jnp.dot(merged.astype(BF16), wo_ref[...], preferred_element_type=F32)


def _merge(oa, ob, proj, gate_block, x2d, wa, wb, wo, tm):
    m, d = x2d.shape
    single = dict(pipeline_mode=pl.Buffered(1))
    return pl.pallas_call(
        _merge_kernel,
        grid=(m // tm,),
        in_specs=[
            pl.BlockSpec((tm, MIX_W), lambda i: (i, 0)),
            pl.BlockSpec((tm, MIX_W), lambda i: (i, 0)),
            pl.BlockSpec((tm, 2 * d), lambda i: (i, gate_block)),
            pl.BlockSpec((tm, d), lambda i: (i, 0)),
            pl.BlockSpec((MIX_W, d), lambda i: (0, 0), **single),
            pl.BlockSpec((MIX_W, d), lambda i: (0, 0), **single),
            pl.BlockSpec((d, d), lambda i: (0, 0), **single),
        ],
        out_specs=pl.BlockSpec((tm, d), lambda i: (i, 0)),
        out_shape=jax.ShapeDtypeStruct((m, d), F32),
        compiler_params=pltpu.CompilerParams(
            dimension_semantics=("arbitrary",), vmem_limit_bytes=VMEM_LIMIT),
        name="merge",
    )(oa, ob, proj, x2d, wa, wb, wo)


def _ffn_kernel(h_ref, wg_ref, wu_ref, wd_ref, nw_ref, fw_ref, o_ref, xn_ref, hid_ref):
    j = pl.program_id(1)
    nf = pl.num_programs(1) - 1

    def gate_up(slot):
        xn = xn_ref[...]
        gate = jnp.dot(xn, wg_ref[...], preferred_element_type=F32)
        up = jnp.dot(xn, wu_ref[...], preferred_element_type=F32)
        hid_ref[slot] = (_silu(gate) * up).astype(BF16)

    def down(slot):
        o_ref[...] += jnp.dot(hid_ref[slot], wd_ref[...], preferred_element_type=F32)

    @pl.when(j == 0)
    def _():
        h = h_ref[...]
        o_ref[...] = h
        xn_ref[...] = _rms_norm_rows(h, nw_ref[...]).astype(BF16)
        gate_up(0)

    @pl.when((j > 0) & (j < nf))
    def _():
        gate_up(j % 2)
        down((j - 1) % 2)

    @pl.when(j == nf)
    def _():
        down((j - 1) % 2)
        o_ref[...] = _rms_norm_rows(o_ref[...], fw_ref[...])


def _ffn(h, w_gate_up, w_down, ffn_norm_w, final_norm_w, tm, tf):
    m, d = h.shape
    ff = w_down.shape[0]
    nf = ff // tf
    return pl.pallas_call(
        _ffn_kernel,
        grid=(m // tm, nf + 1),
        in_specs=[
            pl.BlockSpec((tm, d), lambda i, j: (i, 0)),
            pl.BlockSpec((d, tf), lambda i, j: (0, jnp.minimum(j, nf - 1))),
            pl.BlockSpec((d, tf), lambda i, j: (0, nf + jnp.minimum(j, nf - 1))),
            pl.BlockSpec((tf, d), lambda i, j: (jnp.maximum(j - 1, 0), 0)),
            pl.BlockSpec((1, d), lambda i, j: (0, 0)),
            pl.BlockSpec((1, d), lambda i, j: (0, 0)),
        ],
        out_specs=pl.BlockSpec((tm, d), lambda i, j: (i, 0)),
        out_shape=jax.ShapeDtypeStruct((m, d), F32),
        scratch_shapes=[pltpu.VMEM((tm, d), BF16), pltpu.VMEM((2, tm, tf), BF16)],
        compiler_params=pltpu.CompilerParams(
            dimension_semantics=("arbitrary", "arbitrary"), vmem_limit_bytes=VMEM_LIMIT_FFN),
        name="ffn",
    )(h, w_gate_up, w_gate_up, w_down, ffn_norm_w, final_norm_w)


def _pad_lanes(v):
    return jnp.pad(v.astype(F32), (0, LANES - v.shape[0]))[None, :]


def kernel(x, meta_tokens, lb_logits, mix_norm_w, w_in, hg_norm_w, gd_conv_w, gd_a_log, gd_dt_bias,
           gd_norm_w, w_branch_a, w_branch_b, w_out, ffn_norm_w, w_ffn_in, w_ffn_out, final_norm_w):
    bsz, seq, d = x.shape
    layer = 0
    assert mix_norm_w.shape[0] == 1, "single-layer block"
    assert seq % (REC_CHUNKS_PER_STEP * CHUNK) == 0 and meta_tokens.shape[0] == N_META <= CHUNK

    w = w_in[layer]
    n_mix = 8 * MIX_W
    assert n_mix == N_MIX_TILES * MIX_W
    w_bf = w.astype(BF16)
    w_gate = w_bf[:, n_mix + 2 * HEADS:]
    pad = jnp.zeros((d, LANES - HEADS), BF16)
    w_small = jnp.concatenate([w_bf[:, n_mix:n_mix + HEADS], pad,
                               w_bf[:, n_mix + HEADS:n_mix + 2 * HEADS], pad], axis=1)
    norm_w = mix_norm_w[layer][None, :]
    lb_logits = lb_logits.astype(F32)
    rec_params = (hg_norm_w[layer][None, :], gd_conv_w[layer],
                  _pad_lanes(gd_a_log[layer]), _pad_lanes(gd_dt_bias[layer]), gd_norm_w[layer][None, :])

    n_pad = META_CHUNKS * CHUNK - N_META
    x_meta = jnp.concatenate([jnp.zeros((n_pad, d), x.dtype), meta_tokens.astype(x.dtype)], axis=0)
    proj_m, small_m = _in_proj(x_meta, norm_w, w_bf, w_gate, w_small, lb_logits, tm=META_CHUNKS * CHUNK)
    zero_state = jnp.zeros((HEADS, HEAD_DIM, HEAD_DIM), F32)
    zero_tail = jnp.zeros((CONV_TAIL, 3 * MIX_W), F32)
    _, _, shg0, sgd0, tail0 = _recurrence(proj_m[None], small_m[None], *rec_params,
                                          zero_state, zero_state, zero_tail,
                                          n_pad=n_pad, emit_state=True, cps=META_CHUNKS)

    x2d = x.reshape(bsz * seq, d)
    proj, small = _in_proj(x2d, norm_w, w_bf, w_gate, w_small, lb_logits, tm=min(IN_PROJ_ROWS, bsz * seq))
    oa, ob = _recurrence(proj.reshape(bsz, seq, -1), small.reshape(bsz, seq, -1), *rec_params,
                         shg0, sgd0, tail0, n_pad=0, emit_state=False, cps=REC_CHUNKS_PER_STEP)

    h1 = _merge(oa.reshape(bsz * seq, MIX_W), ob.reshape(bsz * seq, MIX_W), proj,
                n_mix // (2 * d), x2d, w_branch_a[layer].astype(BF16), w_branch_b[layer].astype(BF16),
                w_out[layer].astype(BF16), tm=min(MERGE_ROWS, bsz * seq))
    out = _ffn(h1, w_ffn_in[layer].astype(BF16), w_ffn_out[layer].astype(BF16),
               ffn_norm_w[layer][None, :], final_norm_w[None, :], tm=min(FFN_ROWS, bsz * seq), tf=FFN_COLS)
    return out.reshape(bsz, seq, d)
```
